```python
import math
import jax, jax.numpy as jnp
from jax import lax
import numpy as np

D_MODEL = 2048
BATCH = 4
SEQ = 2048
DEPTH = 2
DEC_BATCH = 32
DEC_SEQ = 8
PAST_LEN = 8192
PAGE_SIZE = 128

C_A = D_MODEL // 2
CONV_W = 31
C_B = D_MODEL // 2
DH_B = 64
H_B = C_B // (2 * DH_B)
C_C = D_MODEL // 2
DH_C = 128
H_C = C_C // DH_C
C_WINDOWS = (128, 512, 2048)
C_DILATIONS = (1, 4, 16)
N_GROUPS_C = len(C_WINDOWS)
IN_AB = 3 * C_A + 4 * C_B
IN_C = 3 * N_GROUPS_C * C_C + C_C
N_AB_LAYERS = (DEPTH + 1) // 2
N_C_LAYERS = DEPTH // 2
ALPHA = (2 * DEPTH) ** 0.25
BETA = (8 * DEPTH) ** -0.25
ROPE_THETA = 10000.0
Q_BLOCK = 128
LN_EPS = 1e-5

kernel_name = "hybrid_conv_diffattn_dilated_decode_step"


def layer_norm(x, g, b):
    xf = x.astype(jnp.float32)
    mu = jnp.mean(xf, -1, keepdims=True)
    var = jnp.mean(jnp.square(xf - mu), -1, keepdims=True)
    y = (xf - mu) * lax.rsqrt(var + LN_EPS) * g.astype(jnp.float32) + b.astype(jnp.float32)
    return y.astype(x.dtype)


def rms_norm(x, g):
    xf = x.astype(jnp.float32)
    ms = jnp.mean(jnp.square(xf), -1, keepdims=True)
    return (xf * lax.rsqrt(ms + LN_EPS) * g.astype(jnp.float32)).astype(x.dtype)


def rope(x, pos):
    dh = x.shape[-1]
    inv_freq = ROPE_THETA ** (-jnp.arange(0, dh, 2, dtype=jnp.float32) / dh)
    ang = pos.astype(jnp.float32)[:, None] * inv_freq[None, :]
    ang = jnp.concatenate([ang, ang], -1)[None, :, None, :]
    x1, x2 = x[..., : dh // 2], x[..., dh // 2:]
    rot = jnp.concatenate([-x2, x1], -1)
    y = x.astype(jnp.float32) * jnp.cos(ang) + rot.astype(jnp.float32) * jnp.sin(ang)
    return y.astype(x.dtype)


def diff_lambda(lq1, lk1, lq2, lk2, lam_init):
    e1 = jnp.exp(jnp.sum(lq1.astype(jnp.float32) * lk1.astype(jnp.float32)))
    e2 = jnp.exp(jnp.sum(lq2.astype(jnp.float32) * lk2.astype(jnp.float32)))
    return e1 - e2 + lam_init


def ab_project(x, pos, w_in, b_in):
    B, T = x.shape[:2]
    h = jnp.einsum("btd,dc->btc", x, w_in) + b_in
    cuts = [C_A, 2 * C_A, 3 * C_A, 3 * C_A + C_B, 3 * C_A + 2 * C_B, 3 * C_A + 3 * C_B]
    a_val, a_gate, g_a, q, k, v, g_b = jnp.split(h, cuts, axis=-1)
    glu = a_val * jax.nn.sigmoid(a_gate)
    q = rope(q.reshape(B, T, 2 * H_B, DH_B), pos) * (DH_B ** -0.5)
    k = rope(k.reshape(B, T, 2 * H_B, DH_B), pos)
    v = v.reshape(B, T, H_B, 2 * DH_B)
    return glu, g_a, q, k, v, g_b


def conv_branch(glu, buf, w_dw, b_dw, g, b):
    xpad = jnp.concatenate([buf, glu], axis=1)
    y = lax.conv_general_dilated(xpad, w_dw[:, None, :], window_strides=(1,), padding="VALID",
                                 dimension_numbers=("NWC", "WIO", "NWC"),
                                 feature_group_count=C_A) + b_dw
    y = jax.nn.silu(layer_norm(y, g, b))
    return y, xpad[:, xpad.shape[1] - (CONV_W - 1):]


def diff_softmax_mix(q, k, v, mask, lam):
    s = jnp.einsum("bqhd,bkhd->bhqk", q, k).astype(jnp.float32)
    s = jnp.where(mask[None, None], s, -jnp.inf)
    p = jax.nn.softmax(s, axis=-1)
    b, h2, tq, tk = p.shape
    p = p.reshape(b, h2 // 2, 2, tq, tk)
    pd = p[:, :, 0] - lam * p[:, :, 1]
    return jnp.einsum("bhqk,bkhd->bqhd", pd.astype(v.dtype), v)


def diff_attn_prompt(q, k, v, lam):
    B, T = q.shape[:2]
    nb = T // Q_BLOCK
    qb = q.reshape(B, nb, Q_BLOCK, 2 * H_B, DH_B).swapaxes(0, 1)
    kpos = jnp.arange(T)

    def one_block(args):
        qi, i = args
        qpos = i * Q_BLOCK + jnp.arange(Q_BLOCK)
        return diff_softmax_mix(qi, k, v, kpos[None, :] <= qpos[:, None], lam)

    out = lax.map(one_block, (qb, jnp.arange(nb)))
    return out.swapaxes(0, 1).reshape(B, T, H_B, 2 * DH_B)


def diff_attn_sample(q, k, v, lam, cache_k, cache_v, page_table):
    T = q.shape[1]
    n_past = page_table.shape[1] * PAGE_SIZE
    mask = jnp.concatenate([jnp.ones((T, n_past), bool), jnp.tril(jnp.ones((T, T), bool))], axis=1)

    def one_seq(args):
        pt, qi, ki, vi = args
        kp = cache_k[pt].reshape(n_past, 2 * H_B, DH_B)
        vp = cache_v[pt].reshape(n_past, H_B, 2 * DH_B)
        kk = jnp.concatenate([kp, ki], axis=0)[None]
        vv = jnp.concatenate([vp, vi], axis=0)[None]
        return diff_softmax_mix(qi[None], kk, vv, mask, lam)[0]

    return lax.map(one_seq, (page_table, q, k, v))


def ab_output(conv_out, g_a, attn, g_b, lam_init, subln_g, w_out):
    B, T = conv_out.shape[:2]
    attn = (rms_norm(attn, subln_g) * (1.0 - lam_init)).reshape(B, T, C_B)
    mixed = jnp.concatenate([conv_out * jax.nn.silu(g_a), attn * jax.nn.silu(g_b)], axis=-1)
    return jnp.einsum("btc,cd->btd", mixed, w_out)


def c_project(x, pos, w_in, b_in):
    B, T = x.shape[:2]
    h = jnp.einsum("btd,dc->btc", x, w_in) + b_in
    parts = jnp.split(h, [C_C * i for i in range(1, 3 * N_GROUPS_C + 1)], axis=-1)
    groups = []
    for g in range(N_GROUPS_C):
        q, k, v = (p.reshape(B, T, H_C, DH_C) for p in parts[3 * g: 3 * g + 3])
        groups.append((rope(q, pos) * (DH_C ** -0.5), rope(k, pos), v))
    return groups, parts[-1]


def dilated_prompt(q, k, v, window, dil):
    B, T = q.shape[:2]
    n = window // dil
    lq = T // dil
    qb_len = math.gcd(Q_BLOCK, lq)
    nb = lq // qb_len
    pad = jnp.zeros((B, window, H_C, DH_C), k.dtype)

    def by_residue(a):
        return jnp.concatenate([pad, a], axis=1).reshape(B, n + lq, dil, H_C, DH_C).transpose(0, 2, 1, 3, 4)

    kr, vr = by_residue(k), by_residue(v)
    qr = q.reshape(B, lq, dil, H_C, DH_C).transpose(0, 2, 1, 3, 4).reshape(B, dil, nb, qb_len, H_C, DH_C)
    win = jnp.arange(nb)[:, None] * qb_len + jnp.arange(n + qb_len)[None, :]
    kw, vw = kr[:, :, win], vr[:, :, win]
    i = jnp.arange(qb_len)[:, None]
    j = jnp.arange(n + qb_len)[None, :]
    band = (j >= i) & (j <= i + n)
    mask = band[None] & (win >= n)[:, None, :]
    s = jnp.einsum("brnqhd,brnkhd->brnhqk", qr, kw).astype(jnp.float32)
    s = jnp.where(mask[None, None, :, None], s, -jnp.inf)
    lse = jax.nn.logsumexp(s, axis=-1)
    p = jnp.exp(s - lse[..., None])
    out = jnp.einsum("brnhqk,brnkhd->brnqhd", p.astype(vw.dtype), vw)
    out = out.reshape(B, dil, lq, H_C, DH_C).transpose(0, 2, 1, 3, 4).reshape(B, T, H_C, DH_C)
    lse = lse.transpose(0, 1, 2, 4, 3).reshape(B, dil, lq, H_C).transpose(0, 2, 1, 3).reshape(B, T, H_C)
    return out, lse


def dilated_sample(q, k, v, buf, window, dil):
    B, T = q.shape[:2]
    lb = buf.shape[1]
    n = window // dil
    kk = jnp.concatenate([buf[:, :, 0], k], axis=1)
    vv = jnp.concatenate([buf[:, :, 1], v], axis=1)
    idx = (lb + jnp.arange(T))[:, None] - dil * jnp.arange(n + 1)[None, :]
    valid = idx >= 0
    idx = jnp.maximum(idx, 0)
    kg, vg = kk[:, idx], vv[:, idx]
    s = jnp.einsum("bqhd,bqjhd->bhqj", q, kg).astype(jnp.float32)
    s = jnp.where(valid[None, None], s, -jnp.inf)
    lse = jax.nn.logsumexp(s, axis=-1)
    p = jnp.exp(s - lse[..., None])
    out = jnp.einsum("bhqj,bqjhd->bqhd", p.astype(vg.dtype), vg)
    new_buf = jnp.concatenate([buf, jnp.stack([k, v], axis=2)], axis=1)[:, T:]
    return out, lse.transpose(0, 2, 1), new_buf


def c_output(outs, lses, gate, w_out):
    B, T = gate.shape[:2]
    wts = jax.nn.softmax(jnp.stack(lses, axis=-1), axis=-1).astype(outs[0].dtype)
    o = jnp.einsum("bthdg,bthg->bthd", jnp.stack(outs, axis=-1), wts).reshape(B, T, C_C)
    return jnp.einsum("btc,cd->btd", o * jax.nn.silu(gate), w_out)


def setup_inputs(seed: int = 0) -> dict:
    key = jax.random.key(seed)
    keys = iter(jax.random.split(key, 32))

    def nrm(shape, scale):
        return jax.random.normal(next(keys), shape, jnp.float32) * scale

    n_pages = PAST_LEN // PAGE_SIZE
    n_used = DEC_BATCH * n_pages
    n_phys = (n_used * 5) // 4
    page_table = jax.random.permutation(next(keys), n_phys)[:n_used].reshape(DEC_BATCH, n_pages).astype(jnp.int32)
    return {
        "x_prompt": nrm((BATCH, SEQ, D_MODEL), 1.0),
        "x_sample": nrm((DEC_BATCH, DEC_SEQ, D_MODEL), 1.0),
        "cache_kb": nrm((N_AB_LAYERS, n_phys, PAGE_SIZE, 2 * H_B, DH_B), 1.0),
        "cache_vb": nrm((N_AB_LAYERS, n_phys, PAGE_SIZE, H_B, 2 * DH_B), 1.0),
        "state_conv": nrm((N_AB_LAYERS, DEC_BATCH, CONV_W - 1, C_A), 0.5),
        "state_kv_c0": nrm((N_C_LAYERS, DEC_BATCH, min(C_WINDOWS[0], PAST_LEN), 2, H_C, DH_C), 1.0),
        "state_kv_c1": nrm((N_C_LAYERS, DEC_BATCH, min(C_WINDOWS[1], PAST_LEN), 2, H_C, DH_C), 1.0),
        "state_kv_c2": nrm((N_C_LAYERS, DEC_BATCH, min(C_WINDOWS[2], PAST_LEN), 2, H_C, DH_C), 1.0),
        "page_table": page_table,
        "w_in_ab": nrm((N_AB_LAYERS, D_MODEL, IN_AB), D_MODEL ** -0.5),
        "b_in_ab": nrm((N_AB_LAYERS, IN_AB), 0.02),
        "w_dw": nrm((N_AB_LAYERS, CONV_W, C_A), CONV_W ** -0.5),
        "b_dw": nrm((N_AB_LAYERS, C_A), 0.02),
        "ln_a_g": 1.0 + nrm((N_AB_LAYERS, C_A), 0.05),
        "ln_a_b": nrm((N_AB_LAYERS, C_A), 0.02),
        "lam_q1": nrm((N_AB_LAYERS, DH_B), 0.1),
        "lam_k1": nrm((N_AB_LAYERS, DH_B), 0.1),
        "lam_q2": nrm((N_AB_LAYERS, DH_B), 0.1),
        "lam_k2": nrm((N_AB_LAYERS, DH_B), 0.1),
        "subln_g": 1.0 + nrm((N_AB_LAYERS, 2 * DH_B), 0.05),
        "w_out_ab": nrm((N_AB_LAYERS, C_A + C_B, D_MODEL), BETA * (C_A + C_B) ** -0.5),
        "w_in_c": nrm((N_C_LAYERS, D_MODEL, IN_C), D_MODEL ** -0.5),
        "b_in_c": nrm((N_C_LAYERS, IN_C), 0.02),
        "w_out_c": nrm((N_C_LAYERS, C_C, D_MODEL), BETA * C_C ** -0.5),
        "post_ln_g": 1.0 + nrm((DEPTH, D_MODEL), 0.05),
        "post_ln_b": nrm((DEPTH, D_MODEL), 0.02),
    }


def reference(x_prompt, x_sample, cache_kb, cache_vb, state_conv, state_kv_c0, state_kv_c1, state_kv_c2,
              page_table, w_in_ab, b_in_ab, w_dw, b_dw, ln_a_g, ln_a_b, lam_q1, lam_k1, lam_q2, lam_k2,
              subln_g, w_out_ab, w_in_c, b_in_c, w_out_c, post_ln_g, post_ln_b):
    bp, tp = x_prompt.shape[:2]
    past_len = page_table.shape[1] * PAGE_SIZE
    pos_p = jnp.arange(tp)
    pos_s = past_len + jnp.arange(x_sample.shape[1])
    states_c = (state_kv_c0, state_kv_c1, state_kv_c2)
    xp, xs = x_prompt, x_sample
    kb_p, vb_p, conv_p, kb_s, vb_s, conv_s = [], [], [], [], [], []
    c_p = [[] for _ in range(N_GROUPS_C)]
    c_s = [[] for _ in range(N_GROUPS_C)]
    for layer in range(DEPTH):
        i = layer // 2
        if layer % 2 == 0:
            lam_init = 0.8 - 0.6 * math.exp(-0.3 * layer)
            lam = diff_lambda(lam_q1[i], lam_k1[i], lam_q2[i], lam_k2[i], lam_init)
            glu, g_a, q, k, v, g_b = ab_project(xp, pos_p, w_in_ab[i], b_in_ab[i])
            zero_buf = jnp.zeros((bp, CONV_W - 1, C_A), glu.dtype)
            a_out, buf = conv_branch(glu, zero_buf, w_dw[i], b_dw[i], ln_a_g[i], ln_a_b[i])
            att = diff_attn_prompt(q, k, v, lam)
            fp = ab_output(a_out, g_a, att, g_b, lam_init, subln_g[i], w_out_ab[i])
            kb_p.append(k); vb_p.append(v); conv_p.append(buf)
            glu, g_a, q, k, v, g_b = ab_project(xs, pos_s, w_in_ab[i], b_in_ab[i])
            a_out, buf = conv_branch(glu, state_conv[i], w_dw[i], b_dw[i], ln_a_g[i], ln_a_b[i])
            att = diff_attn_sample(q, k, v, lam, cache_kb[i], cache_vb[i], page_table)
            fs = ab_output(a_out, g_a, att, g_b, lam_init, subln_g[i], w_out_ab[i])
            kb_s.append(k); vb_s.append(v); conv_s.append(buf)
        else:
            groups, gate = c_project(xp, pos_p, w_in_c[i], b_in_c[i])
            outs, lses = [], []
            for g in range(N_GROUPS_C):
                q, k, v = groups[g]
                o, l = dilated_prompt(q, k, v, C_WINDOWS[g], C_DILATIONS[g])
                outs.append(o); lses.append(l)
                c_p[g].append(jnp.stack([k, v], axis=2)[:, tp - min(C_WINDOWS[g], tp):])
            fp = c_output(outs, lses, gate, w_out_c[i])
            groups, gate = c_project(xs, pos_s, w_in_c[i], b_in_c[i])
            outs, lses = [], []
            for g in range(N_GROUPS_C):
                q, k, v = groups[g]
                o, l, nbuf = dilated_sample(q, k, v, states_c[g][i], C_WINDOWS[g], C_DILATIONS[g])
                outs.append(o); lses.append(l)
                c_s[g].append(nbuf)
            fs = c_output(outs, lses, gate, w_out_c[i])
        xp = layer_norm(ALPHA * xp + fp, post_ln_g[layer], post_ln_b[layer])
        xs = layer_norm(ALPHA * xs + fs, post_ln_g[layer], post_ln_b[layer])
    return (xp, xs,
            jnp.stack(kb_p), jnp.stack(vb_p), jnp.stack(conv_p),
            jnp.stack(kb_s), jnp.stack(vb_s), jnp.stack(conv_s),
            jnp.stack(c_p[0]), jnp.stack(c_p[1]), jnp.stack(c_p[2]),
            jnp.stack(c_s[0]), jnp.stack(c_s[1]), jnp.stack(c_s[2]))
```

```python
import functools
import math

import jax
import jax.numpy as jnp
from jax import lax
from jax.experimental import pallas as pl
from jax.experimental.pallas import tpu as pltpu

F32 = jnp.float32
BF16 = jnp.bfloat16

LANES = 128
SEG = 1024
LN_EPS = 1e-5
ROPE_THETA = 10000.0
CONV_W = 31
PAGE = 128
C_WINDOWS = (128, 512, 2048)
C_DILATIONS = (1, 4, 16)
NEG = -1e30
VMEM_LIMIT = 56 * 1024 * 1024


def _cparams(sem):
    return pltpu.CompilerParams(dimension_semantics=sem, vmem_limit_bytes=VMEM_LIMIT)


def _dot(a, b):
    return jnp.dot(a, b, preferred_element_type=F32)


def _dot_nt(a, b):
    return lax.dot_general(a, b, (((1,), (1,)), ((), ())), preferred_element_type=F32)


def _dot_tn(a, b):
    return lax.dot_general(a, b, (((0,), (0,)), ((), ())), preferred_element_type=F32)


def _silu(x):
    return x * jax.nn.sigmoid(x)


def _rope_rotate(y, dh):
    half = dh // 2
    lane = lax.broadcasted_iota(jnp.int32, y.shape, 1)
    first = (lane & (dh - 1)) < half
    fwd = pltpu.roll(y, SEG - half, axis=1)
    bwd = pltpu.roll(y, half, axis=1)
    return jnp.where(first, -fwd, bwd)


def _proj_body(*refs, kinds, dh, scale):
    nseg = len(kinds)
    x_ref = refs[0]
    w_refs = refs[1:1 + nseg]
    b_refs = refs[1 + nseg:1 + 2 * nseg]
    rest = refs[1 + 2 * nseg:]
    if "rope" in kinds or "rope_q" in kinds:
        cos_ref, sin_ref, o_ref = rest
    else:
        (o_ref,) = rest
    x = x_ref[...]
    if kinds == ("glu_val", "glu_gate"):
        val = _dot(x, w_refs[0][...]) + b_refs[0][...]
        gate = _dot(x, w_refs[1][...]) + b_refs[1][...]
        o_ref[...] = (val * jax.nn.sigmoid(gate)).astype(o_ref.dtype)
        return
    for s, kind in enumerate(kinds):
        y = _dot(x, w_refs[s][...]) + b_refs[s][...]
        if kind in ("rope", "rope_q"):
            y = y * cos_ref[...] + _rope_rotate(y, dh) * sin_ref[...]
            if kind == "rope_q":
                y = y * scale
        elif kind == "silu":
            y = _silu(y)
        o_ref[:, s * SEG:(s + 1) * SEG] = y.astype(o_ref.dtype)


def _proj(x, w, b, seg0, kinds, out_dtype, tm, tables=None, dh=64, name="proj"):
    n, d = x.shape
    nseg = len(kinds)
    n_out = 1 if kinds == ("glu_val", "glu_gate") else nseg
    in_specs = [pl.BlockSpec((tm, d), lambda i: (i, 0))]
    args = [x]
    for s in range(nseg):
        in_specs.append(pl.BlockSpec((d, SEG), lambda i, s=s: (0, seg0 + s)))
        args.append(w)
    for s in range(nseg):
        in_specs.append(pl.BlockSpec((1, SEG), lambda i, s=s: (0, seg0 + s)))
        args.append(b)
    if tables is not None:
        cos, sin = tables
        nrep = cos.shape[0] // tm
        for t in (cos, sin):
            in_specs.append(pl.BlockSpec((tm, SEG), lambda i: (i % nrep, 0)))
            args.append(t)
    body = functools.partial(_proj_body, kinds=kinds, dh=dh, scale=dh ** -0.5)
    return pl.pallas_call(
        body,
        grid=(n // tm,),
        in_specs=in_specs,
        out_specs=pl.BlockSpec((tm, n_out * SEG), lambda i: (i, 0)),
        out_shape=jax.ShapeDtypeStruct((n, n_out * SEG), out_dtype),
        compiler_params=_cparams(("parallel",)),
        name=name,
    )(*args)


def _rope_tables(pos, dh):
    inv_freq = ROPE_THETA ** (-jnp.arange(0, dh, 2, dtype=F32) / dh)
    ang = pos.astype(F32)[:, None] * inv_freq[None, :]
    ang = jnp.concatenate([ang, ang], -1)
    reps = SEG // dh
    return jnp.tile(jnp.cos(ang), (1, reps)), jnp.tile(jnp.sin(ang), (1, reps))


def _conv_finish(acc, g_ref, b_ref, gate):
    mu = jnp.mean(acc, -1, keepdims=True)
    cen = acc - mu
    var = jnp.mean(cen * cen, -1, keepdims=True)
    y = cen * lax.rsqrt(var + LN_EPS) * g_ref[...] + b_ref[...]
    return _silu(y) * gate


def _conv_prompt_body(cur_ref, halo_ref, gate_ref, w_ref, bdw_ref, g_ref, b_ref, o_ref, xp_ref, *, tt, hb):
    i = pl.program_id(1)
    halo = halo_ref[...]
    xp_ref[0:hb, :] = jnp.where(i == 0, 0.0, halo)
    xp_ref[hb:hb + tt, :] = cur_ref[...]
    off = hb - (CONV_W - 1)
    acc = jnp.zeros((tt, SEG), F32) + bdw_ref[...]
    for j in range(CONV_W):
        acc = acc + w_ref[j:j + 1, :] * xp_ref[off + j:off + j + tt, :]
    o_ref[...] = _conv_finish(acc, g_ref, b_ref, gate_ref[...]).astype(o_ref.dtype)


def _conv_prompt(glu, gate, w_dw, b_dw, g, b, bsz, t):
    tt, hb = 256, 32
    glu3 = glu.reshape(bsz, t, SEG)
    gate3 = gate.reshape(bsz, t, SEG)
    vec = lambda: pl.BlockSpec((1, SEG), lambda bi, i: (0, 0))
    out = pl.pallas_call(
        functools.partial(_conv_prompt_body, tt=tt, hb=hb),
        grid=(bsz, t // tt),
        in_specs=[
            pl.BlockSpec((None, tt, SEG), lambda bi, i: (bi, i, 0)),
            pl.BlockSpec((None, hb, SEG), lambda bi, i: (bi, jnp.maximum(i * (tt // hb) - 1, 0), 0)),
            pl.BlockSpec((None, tt, SEG), lambda bi, i: (bi, i, 0)),
            pl.BlockSpec((CONV_W, SEG), lambda bi, i: (0, 0)),
            vec(), vec(), vec(),
        ],
        out_specs=pl.BlockSpec((None, tt, SEG), lambda bi, i: (bi, i, 0)),
        out_shape=jax.ShapeDtypeStruct((bsz, t, SEG), BF16),
        scratch_shapes=[pltpu.VMEM((hb + tt, SEG), F32)],
        compiler_params=_cparams(("parallel", "parallel")),
        name="conv_prompt",
    )(glu3, glu3, gate3, w_dw, b_dw, g, b)
    return out.reshape(bsz * t, SEG)


def _conv_sample_body(xp_ref, gate_ref, w_ref, bdw_ref, g_ref, b_ref, o_ref, *, ts):
    acc = jnp.zeros((ts, SEG), F32) + bdw_ref[...]
    for j in range(CONV_W):
        acc = acc + w_ref[j:j + 1, :] * xp_ref[j:j + ts, :]
    o_ref[...] = _conv_finish(acc, g_ref, b_ref, gate_ref[...]).astype(o_ref.dtype)


def _conv_sample(xpad, gate, w_dw, b_dw, g, b):
    bsz, rows, _ = xpad.shape
    ts = rows - (CONV_W - 1)
    gate3 = gate.reshape(bsz, ts, SEG)
    vec = lambda: pl.BlockSpec((1, SEG), lambda bi: (0, 0))
    out = pl.pallas_call(
        functools.partial(_conv_sample_body, ts=ts),
        grid=(bsz,),
        in_specs=[
            pl.BlockSpec((None, rows, SEG), lambda bi: (bi, 0, 0)),
            pl.BlockSpec((None, ts, SEG), lambda bi: (bi, 0, 0)),
            pl.BlockSpec((CONV_W, SEG), lambda bi: (0, 0)),
            vec(), vec(), vec(),
        ],
        out_specs=pl.BlockSpec((None, ts, SEG), lambda bi: (bi, 0, 0)),
        out_shape=jax.ShapeDtypeStruct((bsz, ts, SEG), BF16),
        compiler_params=_cparams(("parallel",)),
        name="conv_sample",
    )(xpad, gate3, w_dw, b_dw, g, b)
    return out.reshape(bsz * ts, SEG)


def _flash(q, k_ref, v_ref, qpos, lo, hi, tk, mask_fn, m_ref, l_ref, acc_ref):
    m_ref[...] = jnp.full(m_ref.shape, NEG, F32)
    l_ref[...] = jnp.zeros(l_ref.shape, F32)
    acc_ref[...] = jnp.zeros(acc_ref.shape, F32)

    def step(j, carry):
        start = pl.multiple_of(j * tk, tk)
        kb = k_ref[pl.ds(start, tk), :].astype(BF16)
        vb = v_ref[pl.ds(start, tk), :].astype(BF16)
        s = _dot_nt(q, kb)
        kpos = start + lax.broadcasted_iota(jnp.int32, (1, tk), 1)
        s = jnp.where(mask_fn(qpos, kpos), s, NEG)
        m_old = m_ref[...]
        m_new = jnp.maximum(m_old, jnp.max(s, -1, keepdims=True))
        alpha = jnp.exp(m_old - m_new)
        p = jnp.exp(s - m_new)
        l_ref[...] = alpha * l_ref[...] + jnp.sum(p, -1, keepdims=True)
        acc_ref[...] = alpha * acc_ref[...] + _dot(p.astype(BF16), vb)
        m_ref[...] = m_new
        return carry

    lax.fori_loop(lo, hi, step, 0)


def _diff_lambda(lq1_ref, lk1_ref, lq2_ref, lk2_ref, lam_init):
    e1 = jnp.exp(jnp.sum(lq1_ref[...] * lk1_ref[...], keepdims=True))
    e2 = jnp.exp(jnp.sum(lq2_ref[...] * lk2_ref[...], keepdims=True))
    return e1 - e2 + lam_init


def _sub_norm(att, g_ref, lam_init):
    ms = jnp.mean(att * att, -1, keepdims=True)
    return att * lax.rsqrt(ms + LN_EPS) * g_ref[...] * (1.0 - lam_init)


def _diff_prompt_body(q_ref, k_ref, v_ref, gate_ref, lq1, lk1, lq2, lk2, g_ref, o_ref,
                      m_ref, l_ref, acc_ref, *, tq, lam_init):
    i = pl.program_id(2)
    q = q_ref[...]
    lane = lax.broadcasted_iota(jnp.int32, q.shape, 1)
    zero = jnp.zeros_like(q)
    q2 = jnp.concatenate([jnp.where(lane < 64, q, zero), jnp.where(lane >= 64, q, zero)], axis=0)
    row = lax.broadcasted_iota(jnp.int32, (2 * tq, 1), 0)
    qpos = i * tq + jnp.where(row >= tq, row - tq, row)
    _flash(q2, k_ref, v_ref, qpos, 0, i + 1, tq, lambda qp, kp: kp <= qp, m_ref, l_ref, acc_ref)
    o = acc_ref[...] / l_ref[...]
    lam = _diff_lambda(lq1, lk1, lq2, lk2, lam_init)
    att = o[:tq] - lam * o[tq:]
    o_ref[...] = (_sub_norm(att, g_ref, lam_init) * gate_ref[...]).astype(o_ref.dtype)


def _diff_prompt(q, k, v, gate, lam_refs, subln_g, lam_init, bsz, t):
    tq = 256
    nh = SEG // LANES
    r3 = lambda a: a.reshape(bsz, t, SEG)
    small = lambda w: pl.BlockSpec((1, w), lambda bi, h, i: (0, 0))
    out = pl.pallas_call(
        functools.partial(_diff_prompt_body, tq=tq, lam_init=lam_init),
        grid=(bsz, nh, t // tq),
        in_specs=[
            pl.BlockSpec((None, tq, LANES), lambda bi, h, i: (bi, i, h)),
            pl.BlockSpec((None, t, LANES), lambda bi, h, i: (bi, 0, h)),
            pl.BlockSpec((None, t, LANES), lambda bi, h, i: (bi, 0, h)),
            pl.BlockSpec((None, tq, LANES), lambda bi, h, i: (bi, i, h)),
            small(64), small(64), small(64), small(64), small(LANES),
        ],
        out_specs=pl.BlockSpec((None, tq, LANES), lambda bi, h, i: (bi, i, h)),
        out_shape=jax.ShapeDtypeStruct((bsz, t, SEG), BF16),
        scratch_shapes=[pltpu.VMEM((2 * tq, 1), F32), pltpu.VMEM((2 * tq, 1), F32),
                        pltpu.VMEM((2 * tq, LANES), F32)],
        compiler_params=_cparams(("parallel", "parallel", "arbitrary")),
        name="diff_prompt",
    )(r3(q), r3(k), r3(v), r3(gate), *lam_refs, subln_g)
    return out.reshape(bsz * t, SEG)


def _diag_mask(g, per):
    sub = lax.broadcasted_iota(jnp.int32, (g, LANES), 0)
    lane = lax.broadcasted_iota(jnp.int32, (g, LANES), 1)
    return sub == lane // per


def _diag_row(x, diag):
    return jnp.sum(jnp.where(diag, x, 0.0), axis=0, keepdims=True)


def _rows_init(diag, m_ref, l_ref, acc_ref):
    m_ref[...] = jnp.where(diag, -jnp.inf, 0.0)
    l_ref[...] = jnp.zeros(l_ref.shape, F32)
    acc_ref[...] = jnp.zeros(acc_ref.shape, F32)


def _rows_update(k2, qm, mask3, diag, m_ref, l_ref):
    g = diag.shape[0]
    n_t = k2.shape[0] // g
    s3 = _dot(k2, qm).reshape(n_t, g, LANES)
    s3 = jnp.where(mask3, s3, -jnp.inf)
    m_old = m_ref[...]
    m_new = jnp.maximum(m_old, jnp.max(s3, axis=0))
    m_safe = jnp.where(m_new == -jnp.inf, 0.0, m_new)
    alpha = jnp.exp(m_old - m_safe)
    p3 = jnp.exp(s3 - m_safe[None])
    l_ref[...] = alpha * l_ref[...] + jnp.sum(p3, axis=0)
    m_ref[...] = m_new
    return p3, _diag_row(alpha, diag)


def _diff_sample_body(pt_ref, qm_ref, *refs, npg, n_new, lam_init):
    k_refs = refs[:npg]
    v_refs = refs[npg:2 * npg]
    (knew_ref, vnew_ref, gate_ref, lq1, lk1, lq2, lk2, g_ref, o_ref,
     m_ref, l_ref, acc_ref, p_ref) = refs[2 * npg:]
    c = pl.program_id(1)
    nmap = 16
    diag = _diag_mask(nmap, n_new)
    qm = qm_ref[...]

    @pl.when(c == 0)
    def _():
        _rows_init(diag, m_ref, l_ref, acc_ref)

    def update(k3, v3, mask3):
        n_t = k3.shape[0]
        n = n_t * nmap
        k2 = k3.reshape(n, k3.shape[-1]).astype(BF16)
        p3, a_row = _rows_update(k2, qm, mask3, diag, m_ref, l_ref)
        p_ref[0:n, :] = p3.reshape(n, LANES)
        p2 = p_ref[pl.ds(0, n // 2, stride=2), :] + p_ref[pl.ds(1, n // 2, stride=2), :]
        v2 = v3.reshape(n // 2, LANES).astype(BF16)
        acc_ref[...] = acc_ref[...] * a_row + _dot_tn(v2, p2.astype(BF16))

    for r in range(npg):
        update(k_refs[r][...], v_refs[r][...], diag[None])

    @pl.when(c == pl.num_programs(1) - 1)
    def _():
        tnew = lax.broadcasted_iota(jnp.int32, (n_new, nmap, LANES), 0)
        lane = lax.broadcasted_iota(jnp.int32, (n_new, nmap, LANES), 2)
        causal = tnew <= (lane & (n_new - 1))
        update(knew_ref[...], vnew_ref[...], diag[None] & causal)
        l_row = _diag_row(l_ref[...], diag)
        o = (acc_ref[...] / l_row).T
        lam = _diff_lambda(lq1, lk1, lq2, lk2, lam_init)
        for h in range(nmap // 2):
            o1 = o[(2 * h) * n_new:(2 * h + 1) * n_new]
            o2 = o[(2 * h + 1) * n_new:(2 * h + 2) * n_new]
            att = _sub_norm(o1 - lam * o2, g_ref, lam_init)
            cols = slice(h * LANES, (h + 1) * LANES)
            o_ref[:, cols] = (att * gate_ref[:, cols]).astype(o_ref.dtype)


def _diff_sample(q, kb5, vb5, gate, cache_k, cache_v, page_table, lam_refs, subln_g, lam_init):
    nb, n_pages = page_table.shape
    n_new = kb5.shape[2]
    nmap, dk = kb5.shape[3], kb5.shape[4]
    nh, dv = vb5.shape[3], vb5.shape[4]
    npg = 4
    qm = q.reshape(nb, n_new, nmap, dk).transpose(0, 3, 2, 1).reshape(nb, dk, nmap * n_new)
    gate3 = gate.reshape(nb, n_new, SEG)
    pt = page_table.reshape(-1)
    small = lambda w: pl.BlockSpec((1, w), lambda b, c, pt: (0, 0))

    def page_spec(shape, r):
        return pl.BlockSpec((None, None) + shape,
                            lambda b, c, pt, r=r: (0, pt[b * n_pages + c * npg + r], 0, 0, 0))

    in_specs = [pl.BlockSpec((None, dk, LANES), lambda b, c, pt: (b, 0, 0))]
    in_specs += [page_spec((PAGE, nmap, dk), r) for r in range(npg)]
    in_specs += [page_spec((PAGE, nh, dv), r) for r in range(npg)]
    in_specs += [
        pl.BlockSpec((None, None, n_new, nmap, dk), lambda b, c, pt: (0, b, 0, 0, 0)),
        pl.BlockSpec((None, None, n_new, nh, dv), lambda b, c, pt: (0, b, 0, 0, 0)),
        pl.BlockSpec((None, n_new, SEG), lambda b, c, pt: (b, 0, 0)),
        small(64), small(64), small(64), small(64), small(LANES),
    ]
    out = pl.pallas_call(
        functools.partial(_diff_sample_body, npg=npg, n_new=n_new, lam_init=lam_init),
        grid_spec=pltpu.PrefetchScalarGridSpec(
            num_scalar_prefetch=1,
            grid=(nb, n_pages // npg),
            in_specs=in_specs,
            out_specs=pl.BlockSpec((None, n_new, SEG), lambda b, c, pt: (b, 0, 0)),
            scratch_shapes=[pltpu.VMEM((nmap, LANES), F32), pltpu.VMEM((nmap, LANES), F32),
                            pltpu.VMEM((dv, LANES), F32), pltpu.VMEM((PAGE * nmap, LANES), F32)],
        ),
        out_shape=jax.ShapeDtypeStruct((nb, n_new, SEG), BF16),
        compiler_params=_cparams(("parallel", "arbitrary")),
        name="diff_sample",
    )(pt, qm, *([cache_k] * npg), *([cache_v] * npg), kb5, vb5, gate3, *lam_refs, subln_g)
    return out.reshape(nb * n_new, SEG)


def _out_ln_body(*refs, nmix, alpha):
    mix_refs = refs[:nmix]
    w_ref, x_ref, g_ref, b_ref, o_ref, ob_ref = refs[nmix:]
    f = _dot(mix_refs[0][...], w_ref[0:SEG, :])
    for s in range(1, nmix):
        f = f + _dot(mix_refs[s][...], w_ref[s * SEG:(s + 1) * SEG, :])
    z = alpha * x_ref[...] + f
    mu = jnp.mean(z, -1, keepdims=True)
    cen = z - mu
    var = jnp.mean(cen * cen, -1, keepdims=True)
    y = cen * lax.rsqrt(var + LN_EPS) * g_ref[...] + b_ref[...]
    o_ref[...] = y
    ob_ref[...] = y.astype(BF16)


def _out_ln(mixes, w, x, g, b, alpha, tm, name):
    n, d = x.shape
    nmix = len(mixes)
    in_specs = [pl.BlockSpec((tm, SEG), lambda i: (i, 0)) for _ in mixes]
    in_specs += [
        pl.BlockSpec((nmix * SEG, d), lambda i: (0, 0)),
        pl.BlockSpec((tm, d), lambda i: (i, 0)),
        pl.BlockSpec((1, d), lambda i: (0, 0)),
        pl.BlockSpec((1, d), lambda i: (0, 0)),
    ]
    return pl.pallas_call(
        functools.partial(_out_ln_body, nmix=nmix, alpha=alpha),
        grid=(n // tm,),
        in_specs=in_specs,
        out_specs=[pl.BlockSpec((tm, d), lambda i: (i, 0)), pl.BlockSpec((tm, d), lambda i: (i, 0))],
        out_shape=[jax.ShapeDtypeStruct((n, d), F32), jax.ShapeDtypeStruct((n, d), BF16)],
        compiler_params=_cparams(("parallel",)),
        name=name,
    )(*mixes, w, x, g, b)


def _dilated_prompt_body(q0, q1, q2, k0, v0, k1, v1, k2, v2, gate_ref, o_ref,
                         m_ref, l_ref, acc_ref, *, tq):
    i = pl.program_id(2)
    row = lax.broadcasted_iota(jnp.int32, (tq, 1), 0)
    qpos = i * tq + row
    outs, lses = [], []
    for q_ref, k_ref, v_ref, win, dil in ((q0, k0, v0, C_WINDOWS[0], C_DILATIONS[0]),
                                          (q1, k1, v1, C_WINDOWS[1], C_DILATIONS[1]),
                                          (q2, k2, v2, C_WINDOWS[2], C_DILATIONS[2])):
        def mask_fn(qp, kp, win=win, dil=dil):
            dist = qp - kp
            return (dist >= 0) & (dist <= win) & ((dist & (dil - 1)) == 0)

        lo = jnp.maximum(i * tq - win, 0) // tq
        _flash(q_ref[...], k_ref, v_ref, qpos, lo, i + 1, tq, mask_fn, m_ref, l_ref, acc_ref)
        l = l_ref[...]
        outs.append(acc_ref[...] / l)
        lses.append(m_ref[...] + jnp.log(l))
    top = jnp.maximum(jnp.maximum(lses[0], lses[1]), lses[2])
    ws = [jnp.exp(x - top) for x in lses]
    tot = ws[0] + ws[1] + ws[2]
    o = (outs[0] * ws[0] + outs[1] * ws[1] + outs[2] * ws[2]) / tot
    o_ref[...] = (o * gate_ref[...]).astype(o_ref.dtype)


def _dilated_prompt(qs, kvs, gate, bsz, t):
    tq = 256
    nh = SEG // LANES
    qspec = lambda: pl.BlockSpec((None, tq, LANES), lambda bi, h, i: (bi, i, h))
    kspec = lambda: pl.BlockSpec((None, t, LANES), lambda bi, h, i: (bi, 0, h))
    vspec = lambda: pl.BlockSpec((None, t, LANES), lambda bi, h, i: (bi, 0, nh + h))
    kv3 = [kv.reshape(bsz, t, 2 * SEG) for kv in kvs]
    out = pl.pallas_call(
        functools.partial(_dilated_prompt_body, tq=tq),
        grid=(bsz, nh, t // tq),
        in_specs=[qspec(), qspec(), qspec(),
                  kspec(), vspec(), kspec(), vspec(), kspec(), vspec(),
                  qspec()],
        out_specs=qspec(),
        out_shape=jax.ShapeDtypeStruct((bsz, t, SEG), BF16),
        scratch_shapes=[pltpu.VMEM((tq, 1), F32), pltpu.VMEM((tq, 1), F32), pltpu.VMEM((tq, LANES), F32)],
        compiler_params=_cparams(("parallel", "parallel", "arbitrary")),
        name="dilated_prompt",
    )(*[q.reshape(bsz, t, SEG) for q in qs],
      kv3[0], kv3[0], kv3[1], kv3[1], kv3[2], kv3[2], gate.reshape(bsz, t, SEG))
    return out.reshape(bsz * t, SEG)


def _dilated_sample_body(qc_ref, buf_ref, new_ref, o_ref, lse_ref, m_ref, l_ref, acc_ref,
                         *, rows, n_new, nh, dil):
    c = pl.program_id(1)
    diag = _diag_mask(nh, n_new)
    qc = qc_ref[...]

    @pl.when(c == 0)
    def _():
        _rows_init(diag, m_ref, l_ref, acc_ref)

    def update(k3, v3, mask3):
        n = k3.shape[0] * nh
        k2 = k3.reshape(n, LANES).astype(BF16)
        p3, a_row = _rows_update(k2, qc, mask3, diag, m_ref, l_ref)
        v2 = v3.reshape(n, LANES).astype(BF16)
        acc_ref[...] = acc_ref[...] * a_row + _dot_tn(v2, p3.reshape(n, LANES).astype(BF16))

    rho = c * rows + lax.broadcasted_iota(jnp.int32, (rows, nh, LANES), 0)
    tq = lax.broadcasted_iota(jnp.int32, (rows, nh, LANES), 2) & (n_new - 1)
    seen = (rho >= tq) & (((rho - tq) & (dil - 1)) == 0)
    update(buf_ref[:, 0:nh, :], buf_ref[:, nh:2 * nh, :], diag[None] & seen)

    @pl.when(c == pl.num_programs(1) - 1)
    def _():
        tn = lax.broadcasted_iota(jnp.int32, (n_new, nh, LANES), 0)
        tl = lax.broadcasted_iota(jnp.int32, (n_new, nh, LANES), 2) & (n_new - 1)
        dist = tl - tn
        ok = (dist >= 0) & ((dist & (dil - 1)) == 0)
        update(new_ref[:, 0:nh, :], new_ref[:, nh:2 * nh, :], diag[None] & ok)
        lane = lax.broadcasted_iota(jnp.int32, (1, LANES), 1)
        used = lane < nh * n_new
        l_row = jnp.where(used, _diag_row(l_ref[...], diag), 1.0)
        m_row = jnp.where(used, _diag_row(m_ref[...], diag), 0.0)
        o = (acc_ref[...] / l_row).T
        lse = jnp.broadcast_to(m_row + jnp.log(l_row), (LANES, LANES)).T
        for h in range(nh):
            cols = slice(h * LANES, (h + 1) * LANES)
            o_ref[:, cols] = o[h * n_new:(h + 1) * n_new]
            lse_ref[:, cols] = lse[h * n_new:(h + 1) * n_new]


def _dilated_sample(q, buf4, new4, dil):
    nb, lb = buf4.shape[0], buf4.shape[1]
    n_new = new4.shape[1]
    nh = buf4.shape[2] // 2
    rows = min(lb, 512)
    qc = q.reshape(nb, n_new, nh, LANES).transpose(0, 3, 2, 1).reshape(nb, LANES, nh * n_new)
    qc = jnp.pad(qc, ((0, 0), (0, 0), (0, LANES - nh * n_new)))
    ospec = lambda: pl.BlockSpec((None, n_new, SEG), lambda b, c: (b, 0, 0))
    return pl.pallas_call(
        functools.partial(_dilated_sample_body, rows=rows, n_new=n_new, nh=nh, dil=dil),
        grid=(nb, lb // rows),
        in_specs=[
            pl.BlockSpec((None, LANES, LANES), lambda b, c: (b, 0, 0)),
            pl.BlockSpec((None, rows, 2 * nh, LANES), lambda b, c: (b, c, 0, 0)),
            pl.BlockSpec((None, n_new, 2 * nh, LANES), lambda b, c: (b, 0, 0, 0)),
        ],
        out_specs=[ospec(), ospec()],
        out_shape=[jax.ShapeDtypeStruct((nb, n_new, SEG), F32), jax.ShapeDtypeStruct((nb, n_new, SEG), F32)],
        scratch_shapes=[pltpu.VMEM((nh, LANES), F32), pltpu.VMEM((nh, LANES), F32),
                        pltpu.VMEM((LANES, LANES), F32)],
        compiler_params=_cparams(("parallel", "arbitrary")),
        name=f"dilated_sample_d{dil}",
    )(qc, buf4, new4)


def _merge_sample_body(o0, o1, o2, l0, l1, l2, gate_ref, o_ref):
    a, b, c = l0[...], l1[...], l2[...]
    top = jnp.maximum(jnp.maximum(a, b), c)
    wa, wb, wc = jnp.exp(a - top), jnp.exp(b - top), jnp.exp(c - top)
    o = (o0[...] * wa + o1[...] * wb + o2[...] * wc) / (wa + wb + wc)
    o_ref[...] = (o * gate_ref[...]).astype(o_ref.dtype)


def _merge_sample(outs, lses, gate):
    n = gate.shape[0]
    spec = lambda: pl.BlockSpec((n, SEG), lambda i: (0, 0))
    flat = lambda a: a.reshape(n, SEG)
    return pl.pallas_call(
        _merge_sample_body,
        grid=(1,),
        in_specs=[spec() for _ in range(7)],
        out_specs=spec(),
        out_shape=jax.ShapeDtypeStruct((n, SEG), BF16),
        compiler_params=_cparams(("arbitrary",)),
        name="merge_sample",
    )(*[flat(o) for o in outs], *[flat(l) for l in lses], gate)


def kernel(x_prompt, x_sample, cache_kb, cache_vb, state_conv, state_kv_c0, state_kv_c1, state_kv_c2,
           page_table, w_in_ab, b_in_ab, w_dw, b_dw, ln_a_g, ln_a_b, lam_q1, lam_k1, lam_q2, lam_k2,
           subln_g, w_out_ab, w_in_c, b_in_c, w_out_c, post_ln_g, post_ln_b):
    bp, tp, d = x_prompt.shape
    bs, ts, _ = x_sample.shape
    n_p, n_s = bp * tp, bs * ts
    past_len = page_table.shape[1] * PAGE
    depth = post_ln_g.shape[0]
    alpha = (2 * depth) ** 0.25
    states_c = (state_kv_c0, state_kv_c1, state_kv_c2)
    tm_p, tm_s = 256, n_s
    row = lambda a: a.reshape(1, -1)

    pos_p = jnp.arange(tp)
    pos_s = jnp.tile(past_len + jnp.arange(ts), bs)
    tabs = {(64, "p"): _rope_tables(pos_p, 64), (64, "s"): _rope_tables(pos_s, 64),
            (128, "p"): _rope_tables(pos_p, 128), (128, "s"): _rope_tables(pos_s, 128)}

    xp32 = x_prompt.reshape(n_p, d)
    xs32 = x_sample.reshape(n_s, d)
    xp16 = xp32.astype(BF16)
    xs16 = xs32.astype(BF16)

    lam_init = 0.8 - 0.6 * math.exp(-0.3 * 0)
    w_in = w_in_ab[0].astype(BF16)
    b_in = row(b_in_ab[0])
    w_out = w_out_ab[0].astype(BF16)
    lam_refs = [row(lam_q1[0]), row(lam_k1[0]), row(lam_q2[0]), row(lam_k2[0])]
    sub_g = row(subln_g[0])
    conv_args = (w_dw[0], row(b_dw[0]), row(ln_a_g[0]), row(ln_a_b[0]))

    def ab_project(x16, tm, grp):
        t64 = tabs[(64, grp)]
        pr = functools.partial(_proj, x16, w_in, b_in, tm=tm, dh=64)
        glu = pr(seg0=0, kinds=("glu_val", "glu_gate"), out_dtype=F32, name="proj_glu_" + grp)
        g_a = pr(seg0=2, kinds=("silu",), out_dtype=F32, name="proj_ga_" + grp)
        q = pr(seg0=3, kinds=("rope_q",), out_dtype=BF16, tables=t64, name="proj_qb_" + grp)
        k = pr(seg0=4, kinds=("rope",), out_dtype=F32, tables=t64, name="proj_kb_" + grp)
        v = pr(seg0=5, kinds=("none",), out_dtype=F32, name="proj_vb_" + grp)
        g_b = pr(seg0=6, kinds=("silu",), out_dtype=F32, name="proj_gb_" + grp)
        return glu, g_a, q, k, v, g_b

    glu, g_a, q, k, v, g_b = ab_project(xp16, tm_p, "p")
    a_mix = _conv_prompt(glu, g_a, *conv_args, bp, tp)
    b_mix = _diff_prompt(q, k, v, g_b, lam_refs, sub_g, lam_init, bp, tp)
    xp32, xp16 = _out_ln([a_mix, b_mix], w_out, xp32, row(post_ln_g[0]), row(post_ln_b[0]), alpha, tm_p, "out_ab_p")
    kb_p = k.reshape(1, bp, tp, 16, 64)
    vb_p = v.reshape(1, bp, tp, 8, 128)
    conv_p = glu.reshape(bp, tp, SEG)[None, :, tp - (CONV_W - 1):]

    glu, g_a, q, k, v, g_b = ab_project(xs16, tm_s, "s")
    xpad = jnp.concatenate([state_conv[0], glu.reshape(bs, ts, SEG)], axis=1)
    a_mix = _conv_sample(xpad, g_a, *conv_args)
    kb_s = k.reshape(1, bs, ts, 16, 64)
    vb_s = v.reshape(1, bs, ts, 8, 128)
    b_mix = _diff_sample(q, kb_s, vb_s, g_b, cache_kb, cache_vb, page_table, lam_refs, sub_g, lam_init)
    xs32, xs16 = _out_ln([a_mix, b_mix], w_out, xs32, row(post_ln_g[0]), row(post_ln_b[0]), alpha, tm_s, "out_ab_s")
    conv_s = xpad[None, :, ts:]

    w_in = w_in_c[0].astype(BF16)
    b_in = row(b_in_c[0])
    w_out = w_out_c[0].astype(BF16)
    ngrp = len(C_WINDOWS)

    def c_project(x16, tm, grp):
        t128 = tabs[(128, grp)]
        pr = functools.partial(_proj, x16, w_in, b_in, tm=tm, dh=128)
        qs = [pr(seg0=3 * g, kinds=("rope_q",), out_dtype=BF16, tables=t128, name=f"proj_qc{g}_" + grp)
              for g in range(ngrp)]
        kvs = [pr(seg0=3 * g + 1, kinds=("rope", "none"), out_dtype=F32, tables=t128, name=f"proj_kvc{g}_" + grp)
               for g in range(ngrp)]
        gate = pr(seg0=3 * ngrp, kinds=("silu",), out_dtype=F32, name="proj_gc_" + grp)
        return qs, kvs, gate

    qs, kvs, gate = c_project(xp16, tm_p, "p")
    o_mix = _dilated_prompt(qs, kvs, gate, bp, tp)
    y_p, _ = _out_ln([o_mix], w_out, xp32, row(post_ln_g[1]), row(post_ln_b[1]), alpha, tm_p, "out_c_p")
    kvc_p = [kv.reshape(bp, tp, 2, 8, 128)[None, :, tp - min(w, tp):] for kv, w in zip(kvs, C_WINDOWS)]

    qs, kvs, gate = c_project(xs16, tm_s, "s")
    outs, lses, kvc_s = [], [], []
    for g in range(ngrp):
        buf = states_c[g][0]
        lb = buf.shape[1]
        new = kvs[g].reshape(bs, ts, 2, 8, 128)
        o_g, lse_g = _dilated_sample(qs[g], buf.reshape(bs, lb, 16, LANES), new.reshape(bs, ts, 16, LANES),
                                     C_DILATIONS[g])
        outs.append(o_g)
        lses.append(lse_g)
        kvc_s.append(jnp.concatenate([buf, new], axis=1)[None, :, ts:])
    o_mix = _merge_sample(outs, lses, gate)
    y_s, _ = _out_ln([o_mix], w_out, xs32, row(post_ln_g[1]), row(post_ln_b[1]), alpha, tm_s, "out_c_s")

    return (y_p.reshape(bp, tp, d), y_s.reshape(bs, ts, d),
            kb_p, vb_p, conv_p, kb_s, vb_s, conv_s,
            kvc_p[0], kvc_p[1], kvc_p[2], kvc_s[0], kvc_s[1], kvc_s[2])
```

```python
import functools
import math

import jax
import jax.numpy as jnp
from jax import lax
from jax.experimental import pallas as pl
from jax.experimental.pallas import tpu as pltpu

F32 = jnp.float32
BF16 = jnp.bfloat16

LANES = 128
SEG = 1024
LN_EPS = 1e-5
ROPE_THETA = 10000.0
CONV_W = 31
PAGE = 128
C_WINDOWS = (128, 512, 2048)
C_DILATIONS = (1, 4, 16)
NEG = -1e30
VMEM_LIMIT = 56 * 1024 * 1024


def _cparams(sem):
    return pltpu.CompilerParams(dimension_semantics=sem, vmem_limit_bytes=VMEM_LIMIT)


def _dot(a, b):
    return jnp.dot(a, b, preferred_element_type=F32)


def _dot_nt(a, b):
    return lax.dot_general(a, b, (((1,), (1,)), ((), ())), preferred_element_type=F32)


def _dot_tn(a, b):
    return lax.dot_general(a, b, (((0,), (0,)), ((), ())), preferred_element_type=F32)


def _silu(x):
    return x * jax.nn.sigmoid(x)


def _rope_rotate(y, dh):
    half = dh // 2
    lane = lax.broadcasted_iota(jnp.int32, y.shape, 1)
    first = (lane & (dh - 1)) < half
    fwd = pltpu.roll(y, SEG - half, axis=1)
    bwd = pltpu.roll(y, half, axis=1)
    return jnp.where(first, -fwd, bwd)


def _proj_body(*refs, kinds, dh, scale):
    nseg = len(kinds)
    x_ref = refs[0]
    w_refs = refs[1:1 + nseg]
    b_refs = refs[1 + nseg:1 + 2 * nseg]
    rest = refs[1 + 2 * nseg:]
    if "rope" in kinds or "rope_q" in kinds:
        cos_ref, sin_ref, o_ref = rest
    else:
        (o_ref,) = rest
    x = x_ref[...]
    if kinds == ("glu_val", "glu_gate"):
        val = _dot(x, w_refs[0][...]) + b_refs[0][...]
        gate = _dot(x, w_refs[1][...]) + b_refs[1][...]
        o_ref[...] = (val * jax.nn.sigmoid(gate)).astype(o_ref.dtype)
        return
    for s, kind in enumerate(kinds):
        y = _dot(x, w_refs[s][...]) + b_refs[s][...]
        if kind in ("rope", "rope_q"):
            y = y * cos_ref[...] + _rope_rotate(y, dh) * sin_ref[...]
            if kind == "rope_q":
                y = y * scale
        elif kind == "silu":
            y = _silu(y)
        o_ref[:, s * SEG:(s + 1) * SEG] = y.astype(o_ref.dtype)


def _proj(x, w, b, seg0, kinds, out_dtype, tm, tables=None, dh=64, name="proj"):
    n, d = x.shape
    nseg = len(kinds)
    n_out = 1 if kinds == ("glu_val", "glu_gate") else nseg
    in_specs = [pl.BlockSpec((tm, d), lambda i: (i, 0))]
    args = [x]
    for s in range(nseg):
        in_specs.append(pl.BlockSpec((d, SEG), lambda i, s=s: (0, seg0 + s)))
        args.append(w)
    for s in range(nseg):
        in_specs.append(pl.BlockSpec((1, SEG), lambda i, s=s: (0, seg0 + s)))
        args.append(b)
    if tables is not None:
        cos, sin = tables
        nrep = cos.shape[0] // tm
        for t in (cos, sin):
            in_specs.append(pl.BlockSpec((tm, SEG), lambda i: (i % nrep, 0)))
            args.append(t)
    body = functools.partial(_proj_body, kinds=kinds, dh=dh, scale=dh ** -0.5)
    return pl.pallas_call(
        body,
        grid=(n // tm,),
        in_specs=in_specs,
        out_specs=pl.BlockSpec((tm, n_out * SEG), lambda i: (i, 0)),
        out_shape=jax.ShapeDtypeStruct((n, n_out * SEG), out_dtype),
        compiler_params=_cparams(("parallel",)),
        name=name,
    )(*args)


def _rope_angles(pos, dh):
    inv_freq = ROPE_THETA ** (-jnp.arange(0, dh, 2, dtype=F32) / dh)
    ang = pos.astype(F32)[:, None] * inv_freq[None, :]
    return jnp.concatenate([ang, ang], -1)


def _rope_tables(pos, dh):
    ang = _rope_angles(pos, dh)
    reps = SEG // dh
    return jnp.tile(jnp.cos(ang), (1, reps)), jnp.tile(jnp.sin(ang), (1, reps))


def _proj_kt_body(x_ref, wt_ref, b_ref, cos_ref, sin_ref, o_ref, *, dh):
    tm = x_ref.shape[0]
    y = _dot_nt(wt_ref[...], x_ref[...]) + b_ref[...]
    y3 = y.reshape(SEG // dh, dh, tm)
    rot = jnp.concatenate([-y3[:, dh // 2:], y3[:, :dh // 2]], axis=1)
    out = y3 * cos_ref[...][None] + rot * sin_ref[...][None]
    o_ref[...] = out.reshape(SEG, tm)


def _proj_kt(x, wt, bcol, pos, dh, bsz, t, tm, name):
    d = x.shape[1]
    ang = _rope_angles(pos, dh).T
    nt = t // tm
    return pl.pallas_call(
        functools.partial(_proj_kt_body, dh=dh),
        grid=(bsz, nt),
        in_specs=[
            pl.BlockSpec((tm, d), lambda bi, i: (bi * nt + i, 0)),
            pl.BlockSpec((SEG, d), lambda bi, i: (0, 0)),
            pl.BlockSpec((SEG, 1), lambda bi, i: (0, 0)),
            pl.BlockSpec((dh, tm), lambda bi, i: (0, i)),
            pl.BlockSpec((dh, tm), lambda bi, i: (0, i)),
        ],
        out_specs=pl.BlockSpec((None, SEG, tm), lambda bi, i: (bi, 0, i)),
        out_shape=jax.ShapeDtypeStruct((bsz, SEG, t), F32),
        compiler_params=_cparams(("parallel", "parallel")),
        name=name,
    )(x, wt, bcol, jnp.cos(ang), jnp.sin(ang))


def _conv_finish(acc, g_ref, b_ref, gate):
    mu = jnp.mean(acc, -1, keepdims=True)
    cen = acc - mu
    var = jnp.mean(cen * cen, -1, keepdims=True)
    y = cen * lax.rsqrt(var + LN_EPS) * g_ref[...] + b_ref[...]
    return _silu(y) * gate


def _conv_prompt_body(cur_ref, halo_ref, gate_ref, w_ref, bdw_ref, g_ref, b_ref, o_ref, xp_ref, *, tt, hb):
    i = pl.program_id(1)
    halo = halo_ref[...]
    xp_ref[0:hb, :] = jnp.where(i == 0, 0.0, halo)
    xp_ref[hb:hb + tt, :] = cur_ref[...]
    off = hb - (CONV_W - 1)
    acc = jnp.zeros((tt, SEG), F32) + bdw_ref[...]
    for j in range(CONV_W):
        acc = acc + w_ref[j:j + 1, :] * xp_ref[off + j:off + j + tt, :]
    o_ref[...] = _conv_finish(acc, g_ref, b_ref, gate_ref[...]).astype(o_ref.dtype)


def _conv_prompt(glu, gate, w_dw, b_dw, g, b, bsz, t):
    tt, hb = 256, 32
    glu3 = glu.reshape(bsz, t, SEG)
    gate3 = gate.reshape(bsz, t, SEG)
    vec = lambda: pl.BlockSpec((1, SEG), lambda bi, i: (0, 0))
    out = pl.pallas_call(
        functools.partial(_conv_prompt_body, tt=tt, hb=hb),
        grid=(bsz, t // tt),
        in_specs=[
            pl.BlockSpec((None, tt, SEG), lambda bi, i: (bi, i, 0)),
            pl.BlockSpec((None, hb, SEG), lambda bi, i: (bi, jnp.maximum(i * (tt // hb) - 1, 0), 0)),
            pl.BlockSpec((None, tt, SEG), lambda bi, i: (bi, i, 0)),
            pl.BlockSpec((CONV_W, SEG), lambda bi, i: (0, 0)),
            vec(), vec(), vec(),
        ],
        out_specs=pl.BlockSpec((None, tt, SEG), lambda bi, i: (bi, i, 0)),
        out_shape=jax.ShapeDtypeStruct((bsz, t, SEG), BF16),
        scratch_shapes=[pltpu.VMEM((hb + tt, SEG), F32)],
        compiler_params=_cparams(("parallel", "parallel")),
        name="conv_prompt",
    )(glu3, glu3, gate3, w_dw, b_dw, g, b)
    return out.reshape(bsz * t, SEG)


def _conv_sample_body(xp_ref, gate_ref, w_ref, bdw_ref, g_ref, b_ref, o_ref, *, ts):
    acc = jnp.zeros((ts, SEG), F32) + bdw_ref[...]
    for j in range(CONV_W):
        acc = acc + w_ref[j:j + 1, :] * xp_ref[j:j + ts, :]
    o_ref[...] = _conv_finish(acc, g_ref, b_ref, gate_ref[...]).astype(o_ref.dtype)


def _conv_sample(xpad, gate, w_dw, b_dw, g, b):
    bsz, rows, _ = xpad.shape
    ts = rows - (CONV_W - 1)
    gate3 = gate.reshape(bsz, ts, SEG)
    vec = lambda: pl.BlockSpec((1, SEG), lambda bi: (0, 0))
    out = pl.pallas_call(
        functools.partial(_conv_sample_body, ts=ts),
        grid=(bsz,),
        in_specs=[
            pl.BlockSpec((None, rows, SEG), lambda bi: (bi, 0, 0)),
            pl.BlockSpec((None, ts, SEG), lambda bi: (bi, 0, 0)),
            pl.BlockSpec((CONV_W, SEG), lambda bi: (0, 0)),
            vec(), vec(), vec(),
        ],
        out_specs=pl.BlockSpec((None, ts, SEG), lambda bi: (bi, 0, 0)),
        out_shape=jax.ShapeDtypeStruct((bsz, ts, SEG), BF16),
        compiler_params=_cparams(("parallel",)),
        name="conv_sample",
    )(xpad, gate3, w_dw, b_dw, g, b)
    return out.reshape(bsz * ts, SEG)


def _diff_lambda(lq1_ref, lk1_ref, lq2_ref, lk2_ref, lam_init):
    e1 = jnp.exp(jnp.sum(lq1_ref[...] * lk1_ref[...], keepdims=True))
    e2 = jnp.exp(jnp.sum(lq2_ref[...] * lk2_ref[...], keepdims=True))
    return e1 - e2 + lam_init


def _sub_norm(att, g_ref, lam_init):
    ms = jnp.mean(att * att, -1, keepdims=True)
    return att * lax.rsqrt(ms + LN_EPS) * g_ref[...] * (1.0 - lam_init)


def _diff_prompt_body(q_ref, kt_ref, v_ref, gate_ref, lq1, lk1, lq2, lk2, g_ref, o_ref, s_ref,
                      *, tq, ck, lam_init):
    i = pl.program_id(2)
    t = v_ref.shape[0]
    q = q_ref[...]
    lane = lax.broadcasted_iota(jnp.int32, q.shape, 1)
    zero = jnp.zeros_like(q)
    q2 = jnp.concatenate([jnp.where(lane < 64, q, zero), jnp.where(lane >= 64, q, zero)], axis=0)
    row = lax.broadcasted_iota(jnp.int32, (2 * tq, 1), 0)
    qpos = i * tq + jnp.where(row >= tq, row - tq, row)
    tiles_per_chunk = ck // tq

    def attend(nchunk):
        mx = jnp.full((2 * tq, 1), NEG, F32)
        for j in range(nchunk):
            cols = slice(j * ck, (j + 1) * ck)
            s = _dot(q2, kt_ref[:, cols].astype(BF16))
            if j == nchunk - 1:
                kpos = j * ck + lax.broadcasted_iota(jnp.int32, (1, ck), 1)
                s = jnp.where(kpos <= qpos, s, NEG)
            s_ref[:, cols] = s
            mx = jnp.maximum(mx, jnp.max(s, -1, keepdims=True))
        l = jnp.zeros((2 * tq, 1), F32)
        acc = jnp.zeros((2 * tq, LANES), F32)
        for j in range(nchunk):
            cols = slice(j * ck, (j + 1) * ck)
            p = jnp.exp(s_ref[:, cols] - mx)
            l = l + jnp.sum(p, -1, keepdims=True)
            acc = acc + _dot(p.astype(BF16), v_ref[cols, :].astype(BF16))
        o = acc / l
        lam = _diff_lambda(lq1, lk1, lq2, lk2, lam_init)
        att = o[:tq] - lam * o[tq:]
        o_ref[...] = (_sub_norm(att, g_ref, lam_init) * gate_ref[...]).astype(o_ref.dtype)

    for c in range(t // ck):
        pl.when(i // tiles_per_chunk == c)(functools.partial(attend, c + 1))


def _diff_prompt(q, kt, v, gate, lam_refs, subln_g, lam_init, bsz, t):
    tq, ck = 256, 512
    nh = SEG // LANES
    r3 = lambda a: a.reshape(bsz, t, SEG)
    small = lambda w: pl.BlockSpec((1, w), lambda bi, h, i: (0, 0))
    out = pl.pallas_call(
        functools.partial(_diff_prompt_body, tq=tq, ck=ck, lam_init=lam_init),
        grid=(bsz, nh, t // tq),
        in_specs=[
            pl.BlockSpec((None, tq, LANES), lambda bi, h, i: (bi, i, h)),
            pl.BlockSpec((None, LANES, t), lambda bi, h, i: (bi, h, 0)),
            pl.BlockSpec((None, t, LANES), lambda bi, h, i: (bi, 0, h)),
            pl.BlockSpec((None, tq, LANES), lambda bi, h, i: (bi, i, h)),
            small(64), small(64), small(64), small(64), small(LANES),
        ],
        out_specs=pl.BlockSpec((None, tq, LANES), lambda bi, h, i: (bi, i, h)),
        out_shape=jax.ShapeDtypeStruct((bsz, t, SEG), BF16),
        scratch_shapes=[pltpu.VMEM((2 * tq, t), F32)],
        compiler_params=_cparams(("parallel", "parallel", "arbitrary")),
        name="diff_prompt",
    )(r3(q), kt, r3(v), r3(gate), *lam_refs, subln_g)
    return out.reshape(bsz * t, SEG)


def _diff_sample_body(pt_ref, q_ref, *refs, npg, n_new, nmap, lam_init):
    k_refs = refs[:npg]
    v_refs = refs[npg:2 * npg]
    (knew_ref, vnew_ref, gate_ref, lq1, lk1, lq2, lk2, g_ref, o_ref,
     qbd_ref, m_ref, l_ref, acc_ref, kpad_ref, vpad_ref) = refs[2 * npg:]
    c = pl.program_id(1)
    nh = nmap // 2
    dk = SEG // nmap
    rows_h = 2 * n_new

    @pl.when(c == 0)
    def _():
        qt = jnp.tile(q_ref[...].astype(F32), (nmap, 1))
        rmap = lax.broadcasted_iota(jnp.int32, qt.shape, 0) // n_new
        cmap = lax.broadcasted_iota(jnp.int32, qt.shape, 1) // dk
        qbd_ref[...] = jnp.where(rmap == cmap, qt, 0.0).astype(BF16)
        m_ref[...] = jnp.full(m_ref.shape, NEG, F32)
        l_ref[...] = jnp.zeros(l_ref.shape, F32)
        acc_ref[...] = jnp.zeros(acc_ref.shape, F32)

    qbd = qbd_ref[...]

    def update(s, values_of_head):
        m_prev = m_ref[...]
        m_new = jnp.maximum(m_prev, jnp.max(s, -1, keepdims=True))
        alpha = jnp.exp(m_prev - m_new)
        p = jnp.exp(s - m_new)
        l_ref[...] = alpha * l_ref[...] + jnp.sum(p, -1, keepdims=True)
        m_ref[...] = m_new
        pb = p.astype(BF16)
        for h in range(nh):
            rows = slice(h * rows_h, (h + 1) * rows_h)
            acc_ref[rows, :] = alpha[rows] * acc_ref[rows, :] + _dot(pb[rows, :], values_of_head(h))

    s = jnp.concatenate([_dot(qbd, k_refs[r][...].astype(BF16)) for r in range(npg)], axis=1)
    update(s, lambda h: jnp.concatenate(
        [v_refs[r][pl.ds(h, PAGE, stride=nh), :].astype(BF16) for r in range(npg)], axis=0))

    @pl.when(c == pl.num_programs(1) - 1)
    def _():
        kpad_ref[...] = jnp.zeros(kpad_ref.shape, F32)
        vpad_ref[...] = jnp.zeros(vpad_ref.shape, F32)
        kpad_ref[0:n_new, :] = knew_ref[...]
        vpad_ref[0:n_new, :] = vnew_ref[...]
        s_new = _dot_nt(qbd, kpad_ref[...].astype(BF16))
        tnew = lax.broadcasted_iota(jnp.int32, s_new.shape, 1)
        qidx = lax.broadcasted_iota(jnp.int32, s_new.shape, 0) & (n_new - 1)
        s_new = jnp.where(tnew <= qidx, s_new, NEG)
        update(s_new, lambda h: vpad_ref[:, h * LANES:(h + 1) * LANES].astype(BF16))
        o = acc_ref[...] / l_ref[...]
        lam = _diff_lambda(lq1, lk1, lq2, lk2, lam_init)
        for h in range(nh):
            o1 = o[(2 * h) * n_new:(2 * h + 1) * n_new]
            o2 = o[(2 * h + 1) * n_new:(2 * h + 2) * n_new]
            att = _sub_norm(o1 - lam * o2, g_ref, lam_init)
            cols = slice(h * LANES, (h + 1) * LANES)
            o_ref[:, cols] = (att * gate_ref[:, cols]).astype(o_ref.dtype)


def _diff_sample(q, k_new, v_new, gate, cache_kt, cache_v, page_table, lam_refs, subln_g, lam_init, n_new):
    nb, n_pages = page_table.shape
    nmap = 16
    npg = 8
    r3 = lambda a: a.reshape(nb, n_new, SEG)
    pt = page_table.reshape(-1)
    small = lambda w: pl.BlockSpec((1, w), lambda b, c, pt: (0, 0))
    tok = lambda: pl.BlockSpec((None, n_new, SEG), lambda b, c, pt: (b, 0, 0))

    def page_spec(r):
        return pl.BlockSpec((None, SEG, LANES), lambda b, c, pt, r=r: (pt[b * n_pages + c * npg + r], 0, 0))

    in_specs = [tok()]
    in_specs += [page_spec(r) for r in range(npg)]
    in_specs += [page_spec(r) for r in range(npg)]
    in_specs += [tok(), tok(), tok(), small(64), small(64), small(64), small(64), small(LANES)]
    out = pl.pallas_call(
        functools.partial(_diff_sample_body, npg=npg, n_new=n_new, nmap=nmap, lam_init=lam_init),
        grid_spec=pltpu.PrefetchScalarGridSpec(
            num_scalar_prefetch=1,
            grid=(nb, n_pages // npg),
            in_specs=in_specs,
            out_specs=tok(),
            scratch_shapes=[pltpu.VMEM((nmap * n_new, SEG), BF16),
                            pltpu.VMEM((nmap * n_new, 1), F32), pltpu.VMEM((nmap * n_new, 1), F32),
                            pltpu.VMEM((nmap * n_new, LANES), F32),
                            pltpu.VMEM((PAGE, SEG), F32), pltpu.VMEM((PAGE, SEG), F32)],
        ),
        out_shape=jax.ShapeDtypeStruct((nb, n_new, SEG), BF16),
        compiler_params=_cparams(("parallel", "arbitrary")),
        name="diff_sample",
    )(pt, r3(q), *([cache_kt] * npg), *([cache_v] * npg), r3(k_new), r3(v_new), r3(gate), *lam_refs, subln_g)
    return out.reshape(nb * n_new, SEG)


def _out_ln_body(*refs, nmix, alpha):
    mix_refs = refs[:nmix]
    w_ref, x_ref, g_ref, b_ref, o_ref, ob_ref = refs[nmix:]
    f = _dot(mix_refs[0][...], w_ref[0:SEG, :])
    for s in range(1, nmix):
        f = f + _dot(mix_refs[s][...], w_ref[s * SEG:(s + 1) * SEG, :])
    z = alpha * x_ref[...] + f
    mu = jnp.mean(z, -1, keepdims=True)
    cen = z - mu
    var = jnp.mean(cen * cen, -1, keepdims=True)
    y = cen * lax.rsqrt(var + LN_EPS) * g_ref[...] + b_ref[...]
    o_ref[...] = y
    ob_ref[...] = y.astype(BF16)


def _out_ln(mixes, w, x, g, b, alpha, tm, name):
    n, d = x.shape
    nmix = len(mixes)
    in_specs = [pl.BlockSpec((tm, SEG), lambda i: (i, 0)) for _ in mixes]
    in_specs += [
        pl.BlockSpec((nmix * SEG, d), lambda i: (0, 0)),
        pl.BlockSpec((tm, d), lambda i: (i, 0)),
        pl.BlockSpec((1, d), lambda i: (0, 0)),
        pl.BlockSpec((1, d), lambda i: (0, 0)),
    ]
    return pl.pallas_call(
        functools.partial(_out_ln_body, nmix=nmix, alpha=alpha),
        grid=(n // tm,),
        in_specs=in_specs,
        out_specs=[pl.BlockSpec((tm, d), lambda i: (i, 0)), pl.BlockSpec((tm, d), lambda i: (i, 0))],
        out_shape=[jax.ShapeDtypeStruct((n, d), F32), jax.ShapeDtypeStruct((n, d), BF16)],
        compiler_params=_cparams(("parallel",)),
        name=name,
    )(*mixes, w, x, g, b)


def _dilated_prompt_body(q0, q1, q2, k0, v0, k1, v1, k2, v2, gate_ref, o_ref,
                         og0, og1, og2, ls0, ls1, ls2, *, blk):
    t = gate_ref.shape[0]

    def rows(ref, start, n, dil):
        if dil == 1:
            return ref[start:start + n, :]
        return ref[pl.ds(start, n, stride=dil), :]

    def put(ref, start, n, dil, val):
        if dil == 1:
            ref[start:start + n, :] = val
        else:
            ref[pl.ds(start, n, stride=dil), :] = val

    groups = ((q0, k0, v0, og0, ls0, C_WINDOWS[0], C_DILATIONS[0]),
              (q1, k1, v1, og1, ls1, C_WINDOWS[1], C_DILATIONS[1]),
              (q2, k2, v2, og2, ls2, C_WINDOWS[2], C_DILATIONS[2]))
    for q_ref, k_ref, v_ref, og_ref, ls_ref, win, dil in groups:
        assert win // dil == blk
        for r in range(dil):
            for i in range(t // dil // blk):
                base = r + dil * blk * i
                q = rows(q_ref, base, blk, dil).astype(BF16)
                kstart, nk = (r, blk) if i == 0 else (base - dil * blk, 2 * blk)
                kb = rows(k_ref, kstart, nk, dil).astype(BF16)
                vb = rows(v_ref, kstart, nk, dil).astype(BF16)
                s = _dot_nt(q, kb)
                a = lax.broadcasted_iota(jnp.int32, s.shape, 0)
                b = lax.broadcasted_iota(jnp.int32, s.shape, 1)
                ok = (b <= a) if i == 0 else ((b >= a) & (b <= a + blk))
                s = jnp.where(ok, s, NEG)
                m = jnp.max(s, -1, keepdims=True)
                p = jnp.exp(s - m)
                l = jnp.sum(p, -1, keepdims=True)
                put(og_ref, base, blk, dil, _dot(p.astype(BF16), vb) / l)
                put(ls_ref, base, blk, dil, jnp.broadcast_to(m + jnp.log(l), (blk, LANES)))
    la, lb, lc = ls0[...], ls1[...], ls2[...]
    top = jnp.maximum(jnp.maximum(la, lb), lc)
    wa, wb, wc = jnp.exp(la - top), jnp.exp(lb - top), jnp.exp(lc - top)
    o = (og0[...] * wa + og1[...] * wb + og2[...] * wc) / (wa + wb + wc)
    o_ref[...] = (o * gate_ref[...]).astype(o_ref.dtype)


def _dilated_prompt(qs, kvs, gate, bsz, t):
    nh = SEG // LANES
    qspec = lambda: pl.BlockSpec((None, t, LANES), lambda bi, h: (bi, 0, h))
    vspec = lambda: pl.BlockSpec((None, t, LANES), lambda bi, h: (bi, 0, nh + h))
    kv3 = [kv.reshape(bsz, t, 2 * SEG) for kv in kvs]
    out = pl.pallas_call(
        functools.partial(_dilated_prompt_body, blk=128),
        grid=(bsz, nh),
        in_specs=[qspec(), qspec(), qspec(),
                  qspec(), vspec(), qspec(), vspec(), qspec(), vspec(),
                  qspec()],
        out_specs=qspec(),
        out_shape=jax.ShapeDtypeStruct((bsz, t, SEG), BF16),
        scratch_shapes=[pltpu.VMEM((t, LANES), F32) for _ in range(6)],
        compiler_params=_cparams(("parallel", "parallel")),
        name="dilated_prompt",
    )(*[q.reshape(bsz, t, SEG) for q in qs],
      kv3[0], kv3[0], kv3[1], kv3[1], kv3[2], kv3[2], gate.reshape(bsz, t, SEG))
    return out.reshape(bsz * t, SEG)


def _diag_mask(g, per):
    sub = lax.broadcasted_iota(jnp.int32, (g, LANES), 0)
    lane = lax.broadcasted_iota(jnp.int32, (g, LANES), 1)
    return sub == lane // per


def _diag_row(x, diag):
    return jnp.sum(jnp.where(diag, x, 0.0), axis=0, keepdims=True)


def _dilated_sample_body(qc_ref, buf_ref, nxt_ref, new_ref, o_ref, lse_ref, nbuf_ref, m_ref, l_ref, acc_ref,
                         *, rows, n_new, nh, dil):
    c = pl.program_id(1)
    last = c == pl.num_programs(1) - 1
    diag = _diag_mask(nh, n_new)
    qc = qc_ref[...]

    nbuf_ref[0:rows - n_new] = buf_ref[n_new:rows]
    nbuf_ref[rows - n_new:rows] = jnp.where(last, new_ref[...], nxt_ref[...])

    @pl.when(c == 0)
    def _():
        m_ref[...] = jnp.where(diag, -jnp.inf, 0.0)
        l_ref[...] = jnp.zeros(l_ref.shape, F32)
        acc_ref[...] = jnp.zeros(acc_ref.shape, F32)

    def update(k3, v3, mask3):
        n_t = k3.shape[0]
        n = n_t * nh
        s3 = _dot(k3.reshape(n, LANES).astype(BF16), qc).reshape(n_t, nh, LANES)
        s3 = jnp.where(mask3, s3, -jnp.inf)
        m_old = m_ref[...]
        m_new = jnp.maximum(m_old, jnp.max(s3, axis=0))
        m_safe = jnp.where(m_new == -jnp.inf, 0.0, m_new)
        alpha = jnp.exp(m_old - m_safe)
        p3 = jnp.exp(s3 - m_safe[None])
        l_ref[...] = alpha * l_ref[...] + jnp.sum(p3, axis=0)
        m_ref[...] = m_new
        v2 = v3.reshape(n, LANES).astype(BF16)
        acc_ref[...] = acc_ref[...] * _diag_row(alpha, diag) + _dot_tn(v2, p3.reshape(n, LANES).astype(BF16))

    rho = c * rows + lax.broadcasted_iota(jnp.int32, (rows, nh, LANES), 0)
    tq = lax.broadcasted_iota(jnp.int32, (rows, nh, LANES), 2) & (n_new - 1)
    seen = (rho >= tq) & (((rho - tq) & (dil - 1)) == 0)
    update(buf_ref[:, 0:nh, :], buf_ref[:, nh:2 * nh, :], diag[None] & seen)

    @pl.when(last)
    def _():
        tn = lax.broadcasted_iota(jnp.int32, (n_new, nh, LANES), 0)
        tl = lax.broadcasted_iota(jnp.int32, (n_new, nh, LANES), 2) & (n_new - 1)
        dist = tl - tn
        ok = (dist >= 0) & ((dist & (dil - 1)) == 0)
        update(new_ref[:, 0:nh, :], new_ref[:, nh:2 * nh, :], diag[None] & ok)
        lane = lax.broadcasted_iota(jnp.int32, (1, LANES), 1)
        used = lane < nh * n_new
        l_row = jnp.where(used, _diag_row(l_ref[...], diag), 1.0)
        m_row = jnp.where(used, _diag_row(m_ref[...], diag), 0.0)
        o = (acc_ref[...] / l_row).T
        lse = jnp.broadcast_to(m_row + jnp.log(l_row), (LANES, LANES)).T
        for h in range(nh):
            cols = slice(h * LANES, (h + 1) * LANES)
            o_ref[:, cols] = o[h * n_new:(h + 1) * n_new]
            lse_ref[:, cols] = lse[h * n_new:(h + 1) * n_new]


def _dilated_sample(q, buf4, new4, dil):
    nb, lb = buf4.shape[0], buf4.shape[1]
    n_new = new4.shape[1]
    nh = buf4.shape[2] // 2
    rows = min(lb, 512)
    nch = lb // rows
    qc = q.reshape(nb, n_new, nh, LANES).transpose(0, 3, 2, 1).reshape(nb, LANES, nh * n_new)
    qc = jnp.pad(qc, ((0, 0), (0, 0), (0, LANES - nh * n_new)))
    ospec = lambda: pl.BlockSpec((None, n_new, SEG), lambda b, c: (b, 0, 0))
    per = rows // n_new
    return pl.pallas_call(
        functools.partial(_dilated_sample_body, rows=rows, n_new=n_new, nh=nh, dil=dil),
        grid=(nb, nch),
        in_specs=[
            pl.BlockSpec((None, LANES, LANES), lambda b, c: (b, 0, 0)),
            pl.BlockSpec((None, rows, 2 * nh, LANES), lambda b, c: (b, c, 0, 0)),
            pl.BlockSpec((None, n_new, 2 * nh, LANES),
                         lambda b, c: (b, jnp.minimum((c + 1) * per, nch * per - 1), 0, 0)),
            pl.BlockSpec((None, n_new, 2 * nh, LANES), lambda b, c: (b, 0, 0, 0)),
        ],
        out_specs=[ospec(), ospec(),
                   pl.BlockSpec((None, rows, 2 * nh, LANES), lambda b, c: (b, c, 0, 0))],
        out_shape=[jax.ShapeDtypeStruct((nb, n_new, SEG), F32), jax.ShapeDtypeStruct((nb, n_new, SEG), F32),
                   jax.ShapeDtypeStruct(buf4.shape, F32)],
        scratch_shapes=[pltpu.VMEM((nh, LANES), F32), pltpu.VMEM((nh, LANES), F32),
                        pltpu.VMEM((LANES, LANES), F32)],
        compiler_params=_cparams(("parallel", "arbitrary")),
        name=f"dilated_sample_d{dil}",
    )(qc, buf4, buf4, new4)


def _merge_sample_body(o0, o1, o2, l0, l1, l2, gate_ref, o_ref):
    a, b, c = l0[...], l1[...], l2[...]
    top = jnp.maximum(jnp.maximum(a, b), c)
    wa, wb, wc = jnp.exp(a - top), jnp.exp(b - top), jnp.exp(c - top)
    o = (o0[...] * wa + o1[...] * wb + o2[...] * wc) / (wa + wb + wc)
    o_ref[...] = (o * gate_ref[...]).astype(o_ref.dtype)


def _merge_sample(outs, lses, gate):
    n = gate.shape[0]
    spec = lambda: pl.BlockSpec((n, SEG), lambda i: (0, 0))
    flat = lambda a: a.reshape(n, SEG)
    return pl.pallas_call(
        _merge_sample_body,
        grid=(1,),
        in_specs=[spec() for _ in range(7)],
        out_specs=spec(),
        out_shape=jax.ShapeDtypeStruct((n, SEG), BF16),
        compiler_params=_cparams(("arbitrary",)),
        name="merge_sample",
    )(*[flat(o) for o in outs], *[flat(l) for l in lses], gate)


def kernel(x_prompt, x_sample, cache_kb, cache_vb, state_conv, state_kv_c0, state_kv_c1, state_kv_c2,
           page_table, w_in_ab, b_in_ab, w_dw, b_dw, ln_a_g, ln_a_b, lam_q1, lam_k1, lam_q2, lam_k2,
           subln_g, w_out_ab, w_in_c, b_in_c, w_out_c, post_ln_g, post_ln_b):
    bp, tp, d = x_prompt.shape
    bs, ts, _ = x_sample.shape
    n_p, n_s = bp * tp, bs * ts
    n_phys = cache_kb.shape[1]
    past_len = page_table.shape[1] * PAGE
    depth = post_ln_g.shape[0]
    alpha = (2 * depth) ** 0.25
    states_c = (state_kv_c0, state_kv_c1, state_kv_c2)
    tm_p, tm_s = 512, n_s
    row = lambda a: a.reshape(1, -1)

    pos_p = jnp.arange(tp)
    pos_s = jnp.tile(past_len + jnp.arange(ts), bs)
    tabs = {(64, "p"): _rope_tables(pos_p, 64), (64, "s"): _rope_tables(pos_s, 64),
            (128, "p"): _rope_tables(pos_p, 128), (128, "s"): _rope_tables(pos_s, 128)}

    xp32 = x_prompt.reshape(n_p, d)
    xs32 = x_sample.reshape(n_s, d)
    xp16 = xp32.astype(BF16)
    xs16 = xs32.astype(BF16)

    lam_init = 0.8 - 0.6 * math.exp(-0.3 * 0)
    w_in = w_in_ab[0].astype(BF16)
    b_in = row(b_in_ab[0])
    w_out = w_out_ab[0].astype(BF16)
    lam_refs = [row(lam_q1[0]), row(lam_k1[0]), row(lam_q2[0]), row(lam_k2[0])]
    sub_g = row(subln_g[0])
    conv_args = (w_dw[0], row(b_dw[0]), row(ln_a_g[0]), row(ln_a_b[0]))
    kseg = 4

    def ab_project(x16, tm, grp):
        t64 = tabs[(64, grp)]
        pr = functools.partial(_proj, x16, w_in, b_in, tm=tm, dh=64)
        glu = pr(seg0=0, kinds=("glu_val", "glu_gate"), out_dtype=F32, name="proj_glu_" + grp)
        g_a = pr(seg0=2, kinds=("silu",), out_dtype=F32, name="proj_ga_" + grp)
        q = pr(seg0=3, kinds=("rope_q",), out_dtype=BF16, tables=t64, name="proj_qb_" + grp)
        v = pr(seg0=5, kinds=("none",), out_dtype=F32, name="proj_vb_" + grp)
        g_b = pr(seg0=6, kinds=("silu",), out_dtype=F32, name="proj_gb_" + grp)
        return glu, g_a, q, v, g_b

    glu, g_a, q, v, g_b = ab_project(xp16, tm_p, "p")
    kt = _proj_kt(xp16, w_in[:, kseg * SEG:(kseg + 1) * SEG].T, b_in_ab[0, kseg * SEG:(kseg + 1) * SEG].reshape(SEG, 1),
                  pos_p, 64, bp, tp, tm_p, "proj_kbt_p")
    a_mix = _conv_prompt(glu, g_a, *conv_args, bp, tp)
    b_mix = _diff_prompt(q, kt, v, g_b, lam_refs, sub_g, lam_init, bp, tp)
    xp32, xp16 = _out_ln([a_mix, b_mix], w_out, xp32, row(post_ln_g[0]), row(post_ln_b[0]), alpha, tm_p, "out_ab_p")
    kb_p = kt.reshape(1, bp, 16, 64, tp).transpose(0, 1, 4, 2, 3)
    vb_p = v.reshape(1, bp, tp, 8, 128)
    conv_p = glu.reshape(bp, tp, SEG)[None, :, tp - (CONV_W - 1):]

    glu, g_a, q, v, g_b = ab_project(xs16, tm_s, "s")
    k = _proj(xs16, w_in, b_in, seg0=kseg, kinds=("rope",), out_dtype=F32, tm=tm_s, tables=tabs[(64, "s")],
              dh=64, name="proj_kb_s")
    xpad = jnp.concatenate([state_conv[0], glu.reshape(bs, ts, SEG)], axis=1)
    a_mix = _conv_sample(xpad, g_a, *conv_args)
    cache_kt = cache_kb[0].transpose(0, 2, 3, 1).reshape(n_phys, SEG, PAGE)
    cache_v = cache_vb[0].reshape(n_phys, PAGE * 8, LANES)
    b_mix = _diff_sample(q, k, v, g_b, cache_kt, cache_v, page_table, lam_refs, sub_g, lam_init, ts)
    xs32, xs16 = _out_ln([a_mix, b_mix], w_out, xs32, row(post_ln_g[0]), row(post_ln_b[0]), alpha, tm_s, "out_ab_s")
    kb_s = k.reshape(1, bs, ts, 16, 64)
    vb_s = v.reshape(1, bs, ts, 8, 128)
    conv_s = xpad[None, :, ts:]

    w_in = w_in_c[0].astype(BF16)
    b_in = row(b_in_c[0])
    w_out = w_out_c[0].astype(BF16)
    ngrp = len(C_WINDOWS)

    def c_project(x16, tm, grp, q_dtype):
        t128 = tabs[(128, grp)]
        pr = functools.partial(_proj, x16, w_in, b_in, tm=tm, dh=128)
        qs = [pr(seg0=3 * g, kinds=("rope_q",), out_dtype=q_dtype, tables=t128, name=f"proj_qc{g}_" + grp)
              for g in range(ngrp)]
        kvs = [pr(seg0=3 * g + 1, kinds=("rope", "none"), out_dtype=F32, tables=t128, name=f"proj_kvc{g}_" + grp)
               for g in range(ngrp)]
        gate = pr(seg0=3 * ngrp, kinds=("silu",), out_dtype=F32, name="proj_gc_" + grp)
        return qs, kvs, gate

    qs, kvs, gate = c_project(xp16, tm_p, "p", F32)
    o_mix = _dilated_prompt(qs, kvs, gate, bp, tp)
    y_p, _ = _out_ln([o_mix], w_out, xp32, row(post_ln_g[1]), row(post_ln_b[1]), alpha, tm_p, "out_c_p")
    kvc_p = [kv.reshape(bp, tp, 2, 8, 128)[None, :, tp - min(w, tp):] for kv, w in zip(kvs, C_WINDOWS)]

    qs, kvs, gate = c_project(xs16, tm_s, "s", BF16)
    outs, lses, kvc_s = [], [], []
    for g in range(ngrp):
        buf = states_c[g][0]
        lb = buf.shape[1]
        o_g, lse_g, nbuf = _dilated_sample(qs[g], buf.reshape(bs, lb, 16, LANES),
                                           kvs[g].reshape(bs, ts, 16, LANES), C_DILATIONS[g])
        outs.append(o_g)
        lses.append(lse_g)
        kvc_s.append(nbuf.reshape(1, bs, lb, 2, 8, 128))
    o_mix = _merge_sample(outs, lses, gate)
    y_s, _ = _out_ln([o_mix], w_out, xs32, row(post_ln_g[1]), row(post_ln_b[1]), alpha, tm_s, "out_c_s")

    return (y_p.reshape(bp, tp, d), y_s.reshape(bs, ts, d),
            kb_p, vb_p, conv_p, kb_s, vb_s, conv_s,
            kvc_p[0], kvc_p[1], kvc_p[2], kvc_s[0], kvc_s[1], kvc_s[2])
```

```python
import functools
import math

import jax
import jax.numpy as jnp
from jax import lax
from jax.experimental import pallas as pl
from jax.experimental.pallas import tpu as pltpu

F32 = jnp.float32
BF16 = jnp.bfloat16

LANES = 128
SEG = 1024
LN_EPS = 1e-5
ROPE_THETA = 10000.0
CONV_W = 31
PAGE = 128
C_WINDOWS = (128, 512, 2048)
C_DILATIONS = (1, 4, 16)
NEG = -1e30
VMEM_LIMIT = 56 * 1024 * 1024


def _cparams(sem):
    return pltpu.CompilerParams(dimension_semantics=sem, vmem_limit_bytes=VMEM_LIMIT)


def _dot(a, b):
    return jnp.dot(a, b, preferred_element_type=F32)


def _dot_nt(a, b):
    return lax.dot_general(a, b, (((1,), (1,)), ((), ())), preferred_element_type=F32)


def _dot_tn(a, b):
    return lax.dot_general(a, b, (((0,), (0,)), ((), ())), preferred_element_type=F32)


def _silu(x):
    return x * jax.nn.sigmoid(x)


def _rope_rotate(y, dh):
    half = dh // 2
    lane = lax.broadcasted_iota(jnp.int32, y.shape, 1)
    first = (lane & (dh - 1)) < half
    fwd = pltpu.roll(y, SEG - half, axis=1)
    bwd = pltpu.roll(y, half, axis=1)
    return jnp.where(first, -fwd, bwd)


def _proj_body(*refs, kinds, dh, scale):
    nseg = len(kinds)
    x_ref = refs[0]
    w_refs = refs[1:1 + nseg]
    b_refs = refs[1 + nseg:1 + 2 * nseg]
    rest = refs[1 + 2 * nseg:]
    if "rope" in kinds or "rope_q" in kinds:
        cos_ref, sin_ref, o_ref = rest
    else:
        (o_ref,) = rest
    x = x_ref[...]
    if kinds == ("glu_val", "glu_gate"):
        val = _dot(x, w_refs[0][...]) + b_refs[0][...]
        gate = _dot(x, w_refs[1][...]) + b_refs[1][...]
        o_ref[...] = (val * jax.nn.sigmoid(gate)).astype(o_ref.dtype)
        return
    for s, kind in enumerate(kinds):
        y = _dot(x, w_refs[s][...]) + b_refs[s][...]
        if kind in ("rope", "rope_q"):
            y = y * cos_ref[...] + _rope_rotate(y, dh) * sin_ref[...]
            if kind == "rope_q":
                y = y * scale
        elif kind == "silu":
            y = _silu(y)
        o_ref[:, s * SEG:(s + 1) * SEG] = y.astype(o_ref.dtype)


def _proj(x, w, b, seg0, kinds, out_dtype, tm, tables=None, dh=64, name="proj"):
    n, d = x.shape
    nseg = len(kinds)
    n_out = 1 if kinds == ("glu_val", "glu_gate") else nseg
    in_specs = [pl.BlockSpec((tm, d), lambda i: (i, 0))]
    args = [x]
    for s in range(nseg):
        in_specs.append(pl.BlockSpec((d, SEG), lambda i, s=s: (0, seg0 + s)))
        args.append(w)
    for s in range(nseg):
        in_specs.append(pl.BlockSpec((1, SEG), lambda i, s=s: (0, seg0 + s)))
        args.append(b)
    if tables is not None:
        cos, sin = tables
        nrep = cos.shape[0] // tm
        for t in (cos, sin):
            in_specs.append(pl.BlockSpec((tm, SEG), lambda i: (i % nrep, 0)))
            args.append(t)
    body = functools.partial(_proj_body, kinds=kinds, dh=dh, scale=dh ** -0.5)
    return pl.pallas_call(
        body,
        grid=(n // tm,),
        in_specs=in_specs,
        out_specs=pl.BlockSpec((tm, n_out * SEG), lambda i: (i, 0)),
        out_shape=jax.ShapeDtypeStruct((n, n_out * SEG), out_dtype),
        compiler_params=_cparams(("parallel",)),
        name=name,
    )(*args)


def _rope_angles(pos, dh):
    inv_freq = ROPE_THETA ** (-jnp.arange(0, dh, 2, dtype=F32) / dh)
    ang = pos.astype(F32)[:, None] * inv_freq[None, :]
    return jnp.concatenate([ang, ang], -1)


def _rope_tables(pos, dh):
    ang = _rope_angles(pos, dh)
    reps = SEG // dh
    return jnp.tile(jnp.cos(ang), (1, reps)), jnp.tile(jnp.sin(ang), (1, reps))


def _proj_kt_body(x_ref, wt_ref, b_ref, cos_ref, sin_ref, o_ref, *, dh):
    tm = x_ref.shape[0]
    y = _dot_nt(wt_ref[...], x_ref[...]) + b_ref[...]
    y3 = y.reshape(SEG // dh, dh, tm)
    rot = jnp.concatenate([-y3[:, dh // 2:], y3[:, :dh // 2]], axis=1)
    out = y3 * cos_ref[...][None] + rot * sin_ref[...][None]
    o_ref[...] = out.reshape(SEG, tm)


def _proj_kt(x, wt, bcol, pos, dh, bsz, t, tm, name):
    d = x.shape[1]
    ang = _rope_angles(pos, dh).T
    nt = t // tm
    return pl.pallas_call(
        functools.partial(_proj_kt_body, dh=dh),
        grid=(bsz, nt),
        in_specs=[
            pl.BlockSpec((tm, d), lambda bi, i: (bi * nt + i, 0)),
            pl.BlockSpec((SEG, d), lambda bi, i: (0, 0)),
            pl.BlockSpec((SEG, 1), lambda bi, i: (0, 0)),
            pl.BlockSpec((dh, tm), lambda bi, i: (0, i)),
            pl.BlockSpec((dh, tm), lambda bi, i: (0, i)),
        ],
        out_specs=pl.BlockSpec((None, SEG, tm), lambda bi, i: (bi, 0, i)),
        out_shape=jax.ShapeDtypeStruct((bsz, SEG, t), F32),
        compiler_params=_cparams(("parallel", "parallel")),
        name=name,
    )(x, wt, bcol, jnp.cos(ang), jnp.sin(ang))


def _conv_finish(acc, g_ref, b_ref, gate):
    mu = jnp.mean(acc, -1, keepdims=True)
    cen = acc - mu
    var = jnp.mean(cen * cen, -1, keepdims=True)
    y = cen * lax.rsqrt(var + LN_EPS) * g_ref[...] + b_ref[...]
    return _silu(y) * gate


def _conv_prompt_body(cur_ref, halo_ref, gate_ref, w_ref, bdw_ref, g_ref, b_ref, o_ref, xp_ref, xs_ref,
                      *, tt, hb, rb):
    i = pl.program_id(1)
    halo = halo_ref[...]
    xp_ref[0:hb, :] = jnp.where(i == 0, 0.0, halo)
    xp_ref[hb:hb + tt, :] = cur_ref[...]
    span = xs_ref.shape[1]
    for s in range(1, 8):
        xs_ref[s - 1] = xp_ref[s:s + span, :]
    off = hb - (CONV_W - 1)
    for r0 in range(0, tt, rb):
        acc = jnp.zeros((rb // 8, 8, SEG), F32)
        for j in range(CONV_W):
            s, base = (off + j) % 8, (off + j) // 8 * 8 + r0
            x = xp_ref[base:base + rb, :] if s == 0 else xs_ref[s - 1, base:base + rb, :]
            acc = acc + w_ref[j][None] * x.reshape(rb // 8, 8, SEG)
        rows = slice(r0, r0 + rb)
        acc = acc.reshape(rb, SEG) + bdw_ref[...]
        o_ref[rows, :] = _conv_finish(acc, g_ref, b_ref, gate_ref[rows, :]).astype(o_ref.dtype)


def _conv_prompt(glu, gate, w_dw, b_dw, g, b, bsz, t):
    tt, hb, rb = 256, 32, 32
    glu3 = glu.reshape(bsz, t, SEG)
    gate3 = gate.reshape(bsz, t, SEG)
    w_dw = jnp.broadcast_to(w_dw[:, None, :], (CONV_W, 8, SEG))
    vec = lambda: pl.BlockSpec((1, SEG), lambda bi, i: (0, 0))
    out = pl.pallas_call(
        functools.partial(_conv_prompt_body, tt=tt, hb=hb, rb=rb),
        grid=(bsz, t // tt),
        in_specs=[
            pl.BlockSpec((None, tt, SEG), lambda bi, i: (bi, i, 0)),
            pl.BlockSpec((None, hb, SEG), lambda bi, i: (bi, jnp.maximum(i * (tt // hb) - 1, 0), 0)),
            pl.BlockSpec((None, tt, SEG), lambda bi, i: (bi, i, 0)),
            pl.BlockSpec((CONV_W, 8, SEG), lambda bi, i: (0, 0, 0)),
            vec(), vec(), vec(),
        ],
        out_specs=pl.BlockSpec((None, tt, SEG), lambda bi, i: (bi, i, 0)),
        out_shape=jax.ShapeDtypeStruct((bsz, t, SEG), BF16),
        scratch_shapes=[pltpu.VMEM((hb + tt, SEG), F32), pltpu.VMEM((7, hb + tt - 8, SEG), F32)],
        compiler_params=_cparams(("parallel", "parallel")),
        name="conv_prompt",
    )(glu3, glu3, gate3, w_dw, b_dw, g, b)
    return out.reshape(bsz * t, SEG)


def _conv_sample_body(xp_ref, gate_ref, w_ref, bdw_ref, g_ref, b_ref, o_ref, *, ts):
    acc = jnp.zeros((ts, SEG), F32) + bdw_ref[...]
    for j in range(CONV_W):
        acc = acc + w_ref[j:j + 1, :] * xp_ref[j:j + ts, :]
    o_ref[...] = _conv_finish(acc, g_ref, b_ref, gate_ref[...]).astype(o_ref.dtype)


def _conv_sample(xpad, gate, w_dw, b_dw, g, b):
    bsz, rows, _ = xpad.shape
    ts = rows - (CONV_W - 1)
    gate3 = gate.reshape(bsz, ts, SEG)
    vec = lambda: pl.BlockSpec((1, SEG), lambda bi: (0, 0))
    out = pl.pallas_call(
        functools.partial(_conv_sample_body, ts=ts),
        grid=(bsz,),
        in_specs=[
            pl.BlockSpec((None, rows, SEG), lambda bi: (bi, 0, 0)),
            pl.BlockSpec((None, ts, SEG), lambda bi: (bi, 0, 0)),
            pl.BlockSpec((CONV_W, SEG), lambda bi: (0, 0)),
            vec(), vec(), vec(),
        ],
        out_specs=pl.BlockSpec((None, ts, SEG), lambda bi: (bi, 0, 0)),
        out_shape=jax.ShapeDtypeStruct((bsz, ts, SEG), BF16),
        compiler_params=_cparams(("parallel",)),
        name="conv_sample",
    )(xpad, gate3, w_dw, b_dw, g, b)
    return out.reshape(bsz * ts, SEG)


def _diff_lambda(lq1_ref, lk1_ref, lq2_ref, lk2_ref, lam_init):
    e1 = jnp.exp(jnp.sum(lq1_ref[...] * lk1_ref[...], keepdims=True))
    e2 = jnp.exp(jnp.sum(lq2_ref[...] * lk2_ref[...], keepdims=True))
    return e1 - e2 + lam_init


def _sub_norm(att, g_ref, lam_init):
    ms = jnp.mean(att * att, -1, keepdims=True)
    return att * lax.rsqrt(ms + LN_EPS) * g_ref[...] * (1.0 - lam_init)


def _diff_prompt_tile(i, q_ref, kt_ref, v_ref, gate_ref, lam, g_ref, o_ref, s_ref, side_work, *, tq, ck, lam_init):
    t = v_ref.shape[0]
    q = q_ref[...]
    lane = lax.broadcasted_iota(jnp.int32, q.shape, 1)
    zero = jnp.zeros_like(q)
    q2 = jnp.concatenate([jnp.where(lane < 64, q, zero), jnp.where(lane >= 64, q, zero)], axis=0)
    row = lax.broadcasted_iota(jnp.int32, (2 * tq, 1), 0)
    qpos = i * tq + jnp.where(row >= tq, row - tq, row)
    tiles_per_chunk = ck // tq

    def attend(nchunk):
        side_work()
        mx = jnp.full((2 * tq, 1), NEG, F32)
        for j in range(nchunk):
            cols = slice(j * ck, (j + 1) * ck)
            s = _dot(q2, kt_ref[:, cols].astype(BF16))
            if j == nchunk - 1:
                kpos = j * ck + lax.broadcasted_iota(jnp.int32, (1, ck), 1)
                s = jnp.where(kpos <= qpos, s, NEG)
            s_ref[:, cols] = s
            mx = jnp.maximum(mx, jnp.max(s, -1, keepdims=True))
        ones = jnp.ones((ck, LANES), BF16)
        l = jnp.zeros((2 * tq, LANES), F32)
        acc = jnp.zeros((2 * tq, LANES), F32)
        for j in range(nchunk):
            cols = slice(j * ck, (j + 1) * ck)
            p = jnp.exp(s_ref[:, cols] - mx).astype(BF16)
            l = l + _dot(p, ones)
            acc = acc + _dot(p, v_ref[cols, :].astype(BF16))
        o = acc / l
        att = o[:tq] - lam * o[tq:]
        o_ref[...] = (_sub_norm(att, g_ref, lam_init) * gate_ref[...]).astype(o_ref.dtype)

    for c in range(t // ck):
        pl.when(i // tiles_per_chunk == c)(functools.partial(attend, c + 1))


def _diff_attention_body(pt_ref, q_ref, *refs, npg, n_new, nmap, tq, ck, lam_init):
    k_refs = refs[:npg]
    v_refs = refs[npg:2 * npg]
    (knew_ref, vnew_ref, gate_ref, qp_ref, kt_ref, vp_ref, gatep_ref, lq1, lk1, lq2, lk2, g_ref,
     o_ref, op_ref, qbd_ref, m_ref, l_ref, acc_ref, kpad_ref, vpad_ref, s_ref) = refs[2 * npg:]
    c = pl.program_id(1)
    nh = nmap // 2
    dk = SEG // nmap
    rows_h = 2 * n_new
    lam = _diff_lambda(lq1, lk1, lq2, lk2, lam_init)

    @pl.when(c == 0)
    def _():
        qt = jnp.tile(q_ref[...].astype(F32), (nmap, 1))
        rmap = lax.broadcasted_iota(jnp.int32, qt.shape, 0) // n_new
        cmap = lax.broadcasted_iota(jnp.int32, qt.shape, 1) // dk
        qbd_ref[...] = jnp.where(rmap == cmap, qt, 0.0).astype(BF16)
        m_ref[...] = jnp.full(m_ref.shape, NEG, F32)
        l_ref[...] = jnp.zeros(l_ref.shape, F32)
        acc_ref[...] = jnp.zeros(acc_ref.shape, F32)

    def update(s, values_of_head):
        m_prev = m_ref[...]
        m_new = jnp.maximum(m_prev, jnp.max(s, -1, keepdims=True))
        alpha = jnp.exp(m_prev - m_new)
        p = jnp.exp(s - m_new)
        l_ref[...] = alpha * l_ref[...] + jnp.sum(p, -1, keepdims=True)
        m_ref[...] = m_new
        pb = p.astype(BF16)
        for h in range(nh):
            rows = slice(h * rows_h, (h + 1) * rows_h)
            acc_ref[rows, :] = alpha[rows] * acc_ref[rows, :] + _dot(pb[rows, :], values_of_head(h))

    def cache_chunk():
        qbd = qbd_ref[...]
        s = jnp.concatenate([_dot(qbd, k_refs[r][...].astype(BF16)) for r in range(npg)], axis=1)
        update(s, lambda h: jnp.concatenate(
            [v_refs[r][pl.ds(h, PAGE, stride=nh), :].astype(BF16) for r in range(npg)], axis=0))

    _diff_prompt_tile(c, qp_ref, kt_ref, vp_ref, gatep_ref, lam, g_ref, op_ref, s_ref, cache_chunk,
                      tq=tq, ck=ck, lam_init=lam_init)

    @pl.when(c == pl.num_programs(1) - 1)
    def _():
        qbd = qbd_ref[...]
        kpad_ref[...] = jnp.zeros(kpad_ref.shape, F32)
        vpad_ref[...] = jnp.zeros(vpad_ref.shape, F32)
        kpad_ref[0:n_new, :] = knew_ref[...]
        vpad_ref[0:n_new, :] = vnew_ref[...]
        s_new = _dot_nt(qbd, kpad_ref[...].astype(BF16))
        tnew = lax.broadcasted_iota(jnp.int32, s_new.shape, 1)
        qidx = lax.broadcasted_iota(jnp.int32, s_new.shape, 0) & (n_new - 1)
        s_new = jnp.where(tnew <= qidx, s_new, NEG)
        update(s_new, lambda h: vpad_ref[:, h * LANES:(h + 1) * LANES].astype(BF16))
        o = acc_ref[...] / l_ref[...]
        for h in range(nh):
            o1 = o[(2 * h) * n_new:(2 * h + 1) * n_new]
            o2 = o[(2 * h + 1) * n_new:(2 * h + 2) * n_new]
            att = _sub_norm(o1 - lam * o2, g_ref, lam_init)
            cols = slice(h * LANES, (h + 1) * LANES)
            o_ref[:, cols] = (att * gate_ref[:, cols]).astype(o_ref.dtype)


def _diff_attention(q_s, k_new, v_new, gate_s, cache_kt, cache_v, page_table,
                    q_p, kt_p, v_p, gate_p, lam_refs, subln_g, lam_init, n_new, bp, tp):
    nb, n_pages = page_table.shape
    nmap = 16
    npg = 8
    tq, ck = 256, 512
    nh = SEG // LANES
    nchunk = n_pages // npg
    assert bp * nh == nb and tp // tq == nchunk, "one prompt tile per decode chunk"
    r3 = lambda a: a.reshape(nb, n_new, SEG)
    p3 = lambda a: a.reshape(bp, tp, SEG)
    pt = page_table.reshape(-1)
    small = lambda w: pl.BlockSpec((1, w), lambda b, c, pt: (0, 0))
    tok = lambda: pl.BlockSpec((None, n_new, SEG), lambda b, c, pt: (b, 0, 0))
    ptile = lambda: pl.BlockSpec((None, tq, LANES), lambda b, c, pt: (b // nh, c, b % nh))

    def page_spec(r):
        return pl.BlockSpec((None, SEG, LANES), lambda b, c, pt, r=r: (pt[b * n_pages + c * npg + r], 0, 0))

    in_specs = [tok()]
    in_specs += [page_spec(r) for r in range(npg)]
    in_specs += [page_spec(r) for r in range(npg)]
    in_specs += [tok(), tok(), tok(),
                 ptile(),
                 pl.BlockSpec((None, LANES, tp), lambda b, c, pt: (b // nh, b % nh, 0)),
                 pl.BlockSpec((None, tp, LANES), lambda b, c, pt: (b // nh, 0, b % nh)),
                 ptile(),
                 small(64), small(64), small(64), small(64), small(LANES)]
    o_s, o_p = pl.pallas_call(
        functools.partial(_diff_attention_body, npg=npg, n_new=n_new, nmap=nmap, tq=tq, ck=ck, lam_init=lam_init),
        grid_spec=pltpu.PrefetchScalarGridSpec(
            num_scalar_prefetch=1,
            grid=(nb, nchunk),
            in_specs=in_specs,
            out_specs=[tok(), ptile()],
            scratch_shapes=[pltpu.VMEM((nmap * n_new, SEG), BF16),
                            pltpu.VMEM((nmap * n_new, 1), F32), pltpu.VMEM((nmap * n_new, 1), F32),
                            pltpu.VMEM((nmap * n_new, LANES), F32),
                            pltpu.VMEM((PAGE, SEG), F32), pltpu.VMEM((PAGE, SEG), F32),
                            pltpu.VMEM((2 * tq, tp), F32)],
        ),
        out_shape=[jax.ShapeDtypeStruct((nb, n_new, SEG), BF16), jax.ShapeDtypeStruct((bp, tp, SEG), BF16)],
        compiler_params=_cparams(("parallel", "arbitrary")),
        name="diff_attention",
    )(pt, r3(q_s), *([cache_kt] * npg), *([cache_v] * npg), r3(k_new), r3(v_new), r3(gate_s),
      p3(q_p), kt_p, p3(v_p), p3(gate_p), *lam_refs, subln_g)
    return o_p.reshape(bp * tp, SEG), o_s.reshape(nb * n_new, SEG)


def _out_ln_body(*refs, nmix, alpha):
    mix_refs = refs[:nmix]
    w_ref, x_ref, g_ref, b_ref, o_ref, ob_ref = refs[nmix:]
    f = _dot(mix_refs[0][...], w_ref[0:SEG, :])
    for s in range(1, nmix):
        f = f + _dot(mix_refs[s][...], w_ref[s * SEG:(s + 1) * SEG, :])
    z = alpha * x_ref[...] + f
    mu = jnp.mean(z, -1, keepdims=True)
    cen = z - mu
    var = jnp.mean(cen * cen, -1, keepdims=True)
    y = cen * lax.rsqrt(var + LN_EPS) * g_ref[...] + b_ref[...]
    o_ref[...] = y
    ob_ref[...] = y.astype(BF16)


def _out_ln(mixes, w, x, g, b, alpha, tm, name):
    n, d = x.shape
    nmix = len(mixes)
    in_specs = [pl.BlockSpec((tm, SEG), lambda i: (i, 0)) for _ in mixes]
    in_specs += [
        pl.BlockSpec((nmix * SEG, d), lambda i: (0, 0)),
        pl.BlockSpec((tm, d), lambda i: (i, 0)),
        pl.BlockSpec((1, d), lambda i: (0, 0)),
        pl.BlockSpec((1, d), lambda i: (0, 0)),
    ]
    return pl.pallas_call(
        functools.partial(_out_ln_body, nmix=nmix, alpha=alpha),
        grid=(n // tm,),
        in_specs=in_specs,
        out_specs=[pl.BlockSpec((tm, d), lambda i: (i, 0)), pl.BlockSpec((tm, d), lambda i: (i, 0))],
        out_shape=[jax.ShapeDtypeStruct((n, d), F32), jax.ShapeDtypeStruct((n, d), BF16)],
        compiler_params=_cparams(("parallel",)),
        name=name,
    )(*mixes, w, x, g, b)


def _dilated_prompt_body(q0, q1, q2, k0, v0, k1, v1, k2, v2, gate_ref, o_ref,
                         og0, og1, og2, ls0, ls1, ls2, *, blk):
    t = gate_ref.shape[0]

    def rows(ref, start, n, dil):
        if dil == 1:
            return ref[start:start + n, :]
        return ref[pl.ds(start, n, stride=dil), :]

    def put(ref, start, n, dil, val):
        if dil == 1:
            ref[start:start + n, :] = val
        else:
            ref[pl.ds(start, n, stride=dil), :] = val

    groups = ((q0, k0, v0, og0, ls0, C_WINDOWS[0], C_DILATIONS[0]),
              (q1, k1, v1, og1, ls1, C_WINDOWS[1], C_DILATIONS[1]),
              (q2, k2, v2, og2, ls2, C_WINDOWS[2], C_DILATIONS[2]))
    for q_ref, k_ref, v_ref, og_ref, ls_ref, win, dil in groups:
        assert win // dil == blk
        for r in range(dil):
            for i in range(t // dil // blk):
                base = r + dil * blk * i
                q = rows(q_ref, base, blk, dil).astype(BF16)
                kstart, nk = (r, blk) if i == 0 else (base - dil * blk, 2 * blk)
                kb = rows(k_ref, kstart, nk, dil).astype(BF16)
                vb = rows(v_ref, kstart, nk, dil).astype(BF16)
                s = _dot_nt(q, kb)
                a = lax.broadcasted_iota(jnp.int32, s.shape, 0)
                b = lax.broadcasted_iota(jnp.int32, s.shape, 1)
                ok = (b <= a) if i == 0 else ((b >= a) & (b <= a + blk))
                s = jnp.where(ok, s, NEG)
                m = jnp.max(s, -1, keepdims=True)
                p = jnp.exp(s - m)
                l = jnp.sum(p, -1, keepdims=True)
                put(og_ref, base, blk, dil, _dot(p.astype(BF16), vb) / l)
                put(ls_ref, base, blk, dil, jnp.broadcast_to(m + jnp.log(l), (blk, LANES)))
    la, lb, lc = ls0[...], ls1[...], ls2[...]
    top = jnp.maximum(jnp.maximum(la, lb), lc)
    wa, wb, wc = jnp.exp(la - top), jnp.exp(lb - top), jnp.exp(lc - top)
    o = (og0[...] * wa + og1[...] * wb + og2[...] * wc) / (wa + wb + wc)
    o_ref[...] = (o * gate_ref[...]).astype(o_ref.dtype)


def _dilated_prompt(qs, kvs, gate, bsz, t):
    nh = SEG // LANES
    qspec = lambda: pl.BlockSpec((None, t, LANES), lambda bi, h: (bi, 0, h))
    vspec = lambda: pl.BlockSpec((None, t, LANES), lambda bi, h: (bi, 0, nh + h))
    kv3 = [kv.reshape(bsz, t, 2 * SEG) for kv in kvs]
    out = pl.pallas_call(
        functools.partial(_dilated_prompt_body, blk=128),
        grid=(bsz, nh),
        in_specs=[qspec(), qspec(), qspec(),
                  qspec(), vspec(), qspec(), vspec(), qspec(), vspec(),
                  qspec()],
        out_specs=qspec(),
        out_shape=jax.ShapeDtypeStruct((bsz, t, SEG), BF16),
        scratch_shapes=[pltpu.VMEM((t, LANES), F32) for _ in range(6)],
        compiler_params=_cparams(("parallel", "parallel")),
        name="dilated_prompt",
    )(*[q.reshape(bsz, t, SEG) for q in qs],
      kv3[0], kv3[0], kv3[1], kv3[1], kv3[2], kv3[2], gate.reshape(bsz, t, SEG))
    return out.reshape(bsz * t, SEG)


def _diag_mask(g, per):
    sub = lax.broadcasted_iota(jnp.int32, (g, LANES), 0)
    lane = lax.broadcasted_iota(jnp.int32, (g, LANES), 1)
    return sub == lane // per


def _diag_row(x, diag):
    return jnp.sum(jnp.where(diag, x, 0.0), axis=0, keepdims=True)


def _dilated_sample_body(qc_ref, buf_ref, nxt_ref, new_ref, o_ref, lse_ref, nbuf_hbm, m_ref, l_ref, acc_ref, sem,
                         *, rows, n_new, nh, dil):
    b = pl.program_id(0)
    c = pl.program_id(1)
    last = c == pl.num_programs(1) - 1
    diag = _diag_mask(nh, n_new)
    qc = qc_ref[...]

    body_copy = pltpu.make_async_copy(buf_ref.at[0, pl.ds(n_new, rows - n_new)],
                                      nbuf_hbm.at[b, pl.ds(c * rows, rows - n_new)], sem.at[0])
    tail_dst = nbuf_hbm.at[b, pl.ds(c * rows + rows - n_new, n_new)]
    tail_new = pltpu.make_async_copy(new_ref.at[0], tail_dst, sem.at[1])
    tail_nxt = pltpu.make_async_copy(nxt_ref.at[0], tail_dst, sem.at[1])
    body_copy.start()
    pl.when(last)(tail_new.start)
    pl.when(jnp.logical_not(last))(tail_nxt.start)

    @pl.when(c == 0)
    def _():
        m_ref[...] = jnp.where(diag, -jnp.inf, 0.0)
        l_ref[...] = jnp.zeros(l_ref.shape, F32)
        acc_ref[...] = jnp.zeros(acc_ref.shape, F32)

    def update(k3, v3, mask3):
        n_t = k3.shape[0]
        n = n_t * nh
        s3 = _dot(k3.reshape(n, LANES).astype(BF16), qc).reshape(n_t, nh, LANES)
        s3 = jnp.where(mask3, s3, -jnp.inf)
        m_old = m_ref[...]
        m_new = jnp.maximum(m_old, jnp.max(s3, axis=0))
        m_safe = jnp.where(m_new == -jnp.inf, 0.0, m_new)
        alpha = jnp.exp(m_old - m_safe)
        p3 = jnp.exp(s3 - m_safe[None])
        l_ref[...] = alpha * l_ref[...] + jnp.sum(p3, axis=0)
        m_ref[...] = m_new
        v2 = v3.reshape(n, LANES).astype(BF16)
        acc_ref[...] = acc_ref[...] * _diag_row(alpha, diag) + _dot_tn(v2, p3.reshape(n, LANES).astype(BF16))

    k3, v3 = buf_ref[0, :, 0:nh, :], buf_ref[0, :, nh:2 * nh, :]
    idx = lax.broadcasted_iota(jnp.int32, (rows, nh, LANES), 0)
    if dil > n_new:
        keep = lambda a: a.reshape(rows // dil, dil, nh, LANES)[:, 0:n_new].reshape(rows // dil * n_new, nh, LANES)
        k3, v3, idx = keep(k3), keep(v3), idx[0:rows // dil * n_new]
        idx = (idx // n_new) * dil + (idx & (n_new - 1))
    rho = c * rows + idx
    tq = lax.broadcasted_iota(jnp.int32, rho.shape, 2) & (n_new - 1)
    seen = (rho >= tq) & (((rho - tq) & (dil - 1)) == 0)
    update(k3, v3, diag[None] & seen)

    @pl.when(last)
    def _():
        tn = lax.broadcasted_iota(jnp.int32, (n_new, nh, LANES), 0)
        tl = lax.broadcasted_iota(jnp.int32, (n_new, nh, LANES), 2) & (n_new - 1)
        dist = tl - tn
        ok = (dist >= 0) & ((dist & (dil - 1)) == 0)
        update(new_ref[0, :, 0:nh, :], new_ref[0, :, nh:2 * nh, :], diag[None] & ok)
        lane = lax.broadcasted_iota(jnp.int32, (1, LANES), 1)
        used = lane < nh * n_new
        l_row = jnp.where(used, _diag_row(l_ref[...], diag), 1.0)
        m_row = jnp.where(used, _diag_row(m_ref[...], diag), 0.0)
        o = (acc_ref[...] / l_row).T
        lse = jnp.broadcast_to(m_row + jnp.log(l_row), (LANES, LANES)).T
        for h in range(nh):
            cols = slice(h * LANES, (h + 1) * LANES)
            o_ref[:, cols] = o[h * n_new:(h + 1) * n_new]
            lse_ref[:, cols] = lse[h * n_new:(h + 1) * n_new]

    body_copy.wait()
    pl.when(last)(tail_new.wait)
    pl.when(jnp.logical_not(last))(tail_nxt.wait)


def _dilated_sample(q, buf4, new4, dil):
    nb, lb = buf4.shape[0], buf4.shape[1]
    n_new = new4.shape[1]
    nh = buf4.shape[2] // 2
    rows = min(lb, 512)
    nch = lb // rows
    qc = q.reshape(nb, n_new, nh, LANES).transpose(0, 3, 2, 1).reshape(nb, LANES, nh * n_new)
    qc = jnp.pad(qc, ((0, 0), (0, 0), (0, LANES - nh * n_new)))
    ospec = lambda: pl.BlockSpec((None, n_new, SEG), lambda b, c: (b, 0, 0))
    per = rows // n_new
    return pl.pallas_call(
        functools.partial(_dilated_sample_body, rows=rows, n_new=n_new, nh=nh, dil=dil),
        grid=(nb, nch),
        in_specs=[
            pl.BlockSpec((None, LANES, LANES), lambda b, c: (b, 0, 0)),
            pl.BlockSpec((1, rows, 2 * nh, LANES), lambda b, c: (b, c, 0, 0)),
            pl.BlockSpec((1, n_new, 2 * nh, LANES),
                         lambda b, c: (b, jnp.minimum((c + 1) * per, nch * per - 1), 0, 0)),
            pl.BlockSpec((1, n_new, 2 * nh, LANES), lambda b, c: (b, 0, 0, 0)),
        ],
        out_specs=[ospec(), ospec(), pl.BlockSpec(memory_space=pl.ANY)],
        out_shape=[jax.ShapeDtypeStruct((nb, n_new, SEG), F32), jax.ShapeDtypeStruct((nb, n_new, SEG), F32),
                   jax.ShapeDtypeStruct(buf4.shape, F32)],
        scratch_shapes=[pltpu.VMEM((nh, LANES), F32), pltpu.VMEM((nh, LANES), F32),
                        pltpu.VMEM((LANES, LANES), F32), pltpu.SemaphoreType.DMA((2,))],
        compiler_params=_cparams(("parallel", "arbitrary")),
        name=f"dilated_sample_d{dil}",
    )(qc, buf4, buf4, new4)


def _merge_sample_body(o0, o1, o2, l0, l1, l2, gate_ref, o_ref):
    a, b, c = l0[...], l1[...], l2[...]
    top = jnp.maximum(jnp.maximum(a, b), c)
    wa, wb, wc = jnp.exp(a - top), jnp.exp(b - top), jnp.exp(c - top)
    o = (o0[...] * wa + o1[...] * wb + o2[...] * wc) / (wa + wb + wc)
    o_ref[...] = (o * gate_ref[...]).astype(o_ref.dtype)


def _merge_sample(outs, lses, gate):
    n = gate.shape[0]
    spec = lambda: pl.BlockSpec((n, SEG), lambda i: (0, 0))
    flat = lambda a: a.reshape(n, SEG)
    return pl.pallas_call(
        _merge_sample_body,
        grid=(1,),
        in_specs=[spec() for _ in range(7)],
        out_specs=spec(),
        out_shape=jax.ShapeDtypeStruct((n, SEG), BF16),
        compiler_params=_cparams(("arbitrary",)),
        name="merge_sample",
    )(*[flat(o) for o in outs], *[flat(l) for l in lses], gate)


def kernel(x_prompt, x_sample, cache_kb, cache_vb, state_conv, state_kv_c0, state_kv_c1, state_kv_c2,
           page_table, w_in_ab, b_in_ab, w_dw, b_dw, ln_a_g, ln_a_b, lam_q1, lam_k1, lam_q2, lam_k2,
           subln_g, w_out_ab, w_in_c, b_in_c, w_out_c, post_ln_g, post_ln_b):
    bp, tp, d = x_prompt.shape
    bs, ts, _ = x_sample.shape
    n_p, n_s = bp * tp, bs * ts
    n_phys = cache_kb.shape[1]
    past_len = page_table.shape[1] * PAGE
    depth = post_ln_g.shape[0]
    alpha = (2 * depth) ** 0.25
    states_c = (state_kv_c0, state_kv_c1, state_kv_c2)
    tm_p, tm_s = 512, n_s
    row = lambda a: a.reshape(1, -1)

    pos_p = jnp.arange(tp)
    pos_s = jnp.tile(past_len + jnp.arange(ts), bs)
    tabs = {(64, "p"): _rope_tables(pos_p, 64), (64, "s"): _rope_tables(pos_s, 64),
            (128, "p"): _rope_tables(pos_p, 128), (128, "s"): _rope_tables(pos_s, 128)}

    xp32 = x_prompt.reshape(n_p, d)
    xs32 = x_sample.reshape(n_s, d)
    xp16 = xp32.astype(BF16)
    xs16 = xs32.astype(BF16)

    lam_init = 0.8 - 0.6 * math.exp(-0.3 * 0)
    w_in = w_in_ab[0].astype(BF16)
    b_in = row(b_in_ab[0])
    w_out = w_out_ab[0].astype(BF16)
    lam_refs = [row(lam_q1[0]), row(lam_k1[0]), row(lam_q2[0]), row(lam_k2[0])]
    sub_g = row(subln_g[0])
    conv_args = (w_dw[0], row(b_dw[0]), row(ln_a_g[0]), row(ln_a_b[0]))
    kseg = 4

    def ab_project(x16, tm, grp):
        t64 = tabs[(64, grp)]
        pr = functools.partial(_proj, x16, w_in, b_in, tm=tm, dh=64)
        glu = pr(seg0=0, kinds=("glu_val", "glu_gate"), out_dtype=F32, name="proj_glu_" + grp)
        g_a = pr(seg0=2, kinds=("silu",), out_dtype=F32, name="proj_ga_" + grp)
        q = pr(seg0=3, kinds=("rope_q",), out_dtype=BF16, tables=t64, name="proj_qb_" + grp)
        v = pr(seg0=5, kinds=("none",), out_dtype=F32, name="proj_vb_" + grp)
        g_b = pr(seg0=6, kinds=("silu",), out_dtype=F32, name="proj_gb_" + grp)
        return glu, g_a, q, v, g_b

    glu, g_a, q, v, g_b = ab_project(xp16, tm_p, "p")
    kt = _proj_kt(xp16, w_in[:, kseg * SEG:(kseg + 1) * SEG].T, b_in_ab[0, kseg * SEG:(kseg + 1) * SEG].reshape(SEG, 1),
                  pos_p, 64, bp, tp, tm_p, "proj_kbt_p")
    a_mix = _conv_prompt(glu, g_a, *conv_args, bp, tp)
    kb_p = kt.reshape(1, bp, 16, 64, tp).transpose(0, 1, 4, 2, 3)
    vb_p = v.reshape(1, bp, tp, 8, 128)
    conv_p = glu.reshape(bp, tp, SEG)[None, :, tp - (CONV_W - 1):]

    glu, g_a, q_s, v_s, gb_s = ab_project(xs16, tm_s, "s")
    k_s = _proj(xs16, w_in, b_in, seg0=kseg, kinds=("rope",), out_dtype=F32, tm=tm_s, tables=tabs[(64, "s")],
                dh=64, name="proj_kb_s")
    xpad = jnp.concatenate([state_conv[0], glu.reshape(bs, ts, SEG)], axis=1)
    a_mix_s = _conv_sample(xpad, g_a, *conv_args)
    cache_kt = cache_kb[0].transpose(0, 2, 3, 1).reshape(n_phys, SEG, PAGE)
    cache_v = cache_vb[0].reshape(n_phys, PAGE * 8, LANES)
    b_mix, b_mix_s = _diff_attention(q_s, k_s, v_s, gb_s, cache_kt, cache_v, page_table,
                                     q, kt, v, g_b, lam_refs, sub_g, lam_init, ts, bp, tp)
    xp32, xp16 = _out_ln([a_mix, b_mix], w_out, xp32, row(post_ln_g[0]), row(post_ln_b[0]), alpha, tm_p, "out_ab_p")
    xs32, xs16 = _out_ln([a_mix_s, b_mix_s], w_out, xs32, row(post_ln_g[0]), row(post_ln_b[0]), alpha, tm_s,
                         "out_ab_s")
    kb_s = k_s.reshape(1, bs, ts, 16, 64)
    vb_s = v_s.reshape(1, bs, ts, 8, 128)
    conv_s = xpad[None, :, ts:]

    w_in = w_in_c[0].astype(BF16)
    b_in = row(b_in_c[0])
    w_out = w_out_c[0].astype(BF16)
    ngrp = len(C_WINDOWS)

    def c_project(x16, tm, grp, q_dtype):
        t128 = tabs[(128, grp)]
        pr = functools.partial(_proj, x16, w_in, b_in, tm=tm, dh=128)
        qs = [pr(seg0=3 * g, kinds=("rope_q",), out_dtype=q_dtype, tables=t128, name=f"proj_qc{g}_" + grp)
              for g in range(ngrp)]
        kvs = [pr(seg0=3 * g + 1, kinds=("rope", "none"), out_dtype=F32, tables=t128, name=f"proj_kvc{g}_" + grp)
               for g in range(ngrp)]
        gate = pr(seg0=3 * ngrp, kinds=("silu",), out_dtype=F32, name="proj_gc_" + grp)
        return qs, kvs, gate

    qs, kvs, gate = c_project(xp16, tm_p, "p", F32)
    o_mix = _dilated_prompt(qs, kvs, gate, bp, tp)
    y_p, _ = _out_ln([o_mix], w_out, xp32, row(post_ln_g[1]), row(post_ln_b[1]), alpha, tm_p, "out_c_p")
    kvc_p = [kv.reshape(bp, tp, 2, 8, 128)[None, :, tp - min(w, tp):] for kv, w in zip(kvs, C_WINDOWS)]

    qs, kvs, gate = c_project(xs16, tm_s, "s", BF16)
    outs, lses, kvc_s = [], [], []
    for g in range(ngrp):
        buf = states_c[g][0]
        lb = buf.shape[1]
        o_g, lse_g, nbuf = _dilated_sample(qs[g], buf.reshape(bs, lb, 16, LANES),
                                           kvs[g].reshape(bs, ts, 16, LANES), C_DILATIONS[g])
        outs.append(o_g)
        lses.append(lse_g)
        kvc_s.append(nbuf.reshape(1, bs, lb, 2, 8, 128))
    o_mix = _merge_sample(outs, lses, gate)
    y_s, _ = _out_ln([o_mix], w_out, xs32, row(post_ln_g[1]), row(post_ln_b[1]), alpha, tm_s, "out_c_s")

    return (y_p.reshape(bp, tp, d), y_s.reshape(bs, ts, d),
            kb_p, vb_p, conv_p, kb_s, vb_s, conv_s,
            kvc_p[0], kvc_p[1], kvc_p[2], kvc_s[0], kvc_s[1], kvc_s[2])
```

```python
import functools
import math

import jax
import jax.numpy as jnp
from jax import lax
from jax.experimental import pallas as pl
from jax.experimental.pallas import tpu as pltpu

F32 = jnp.float32
BF16 = jnp.bfloat16

LANES = 128
SEG = 1024
LN_EPS = 1e-5
ROPE_THETA = 10000.0
CONV_W = 31
PAGE = 128
C_WINDOWS = (128, 512, 2048)
C_DILATIONS = (1, 4, 16)
NEG = -1e30
VMEM_LIMIT = 56 * 1024 * 1024


def _cparams(sem):
    return pltpu.CompilerParams(dimension_semantics=sem, vmem_limit_bytes=VMEM_LIMIT)


def _dot(a, b):
    return jnp.dot(a, b, preferred_element_type=F32)


def _dot_nt(a, b):
    return lax.dot_general(a, b, (((1,), (1,)), ((), ())), preferred_element_type=F32)


def _dot_tn(a, b):
    return lax.dot_general(a, b, (((0,), (0,)), ((), ())), preferred_element_type=F32)


def _silu(x):
    return x * jax.nn.sigmoid(x)


def _rope_rotate(y, dh):
    half = dh // 2
    lane = lax.broadcasted_iota(jnp.int32, y.shape, 1)
    first = (lane & (dh - 1)) < half
    fwd = pltpu.roll(y, SEG - half, axis=1)
    bwd = pltpu.roll(y, half, axis=1)
    return jnp.where(first, -fwd, bwd)


def _proj_body(*refs, kinds, dh, scale):
    nseg = len(kinds)
    x_ref = refs[0]
    w_refs = refs[1:1 + nseg]
    b_refs = refs[1 + nseg:1 + 2 * nseg]
    rest = refs[1 + 2 * nseg:]
    if "rope" in kinds or "rope_q" in kinds:
        cos_ref, sin_ref, o_ref = rest
    else:
        (o_ref,) = rest
    x = x_ref[...]
    if kinds == ("glu_val", "glu_gate"):
        val = _dot(x, w_refs[0][...]) + b_refs[0][...]
        gate = _dot(x, w_refs[1][...]) + b_refs[1][...]
        o_ref[...] = (val * jax.nn.sigmoid(gate)).astype(o_ref.dtype)
        return
    for s, kind in enumerate(kinds):
        y = _dot(x, w_refs[s][...]) + b_refs[s][...]
        if kind in ("rope", "rope_q"):
            y = y * cos_ref[...] + _rope_rotate(y, dh) * sin_ref[...]
            if kind == "rope_q":
                y = y * scale
        elif kind == "silu":
            y = _silu(y)
        o_ref[:, s * SEG:(s + 1) * SEG] = y.astype(o_ref.dtype)


def _proj(x, w, b, seg0, kinds, out_dtype, tm, tables=None, dh=64, name="proj"):
    n, d = x.shape
    nseg = len(kinds)
    n_out = 1 if kinds == ("glu_val", "glu_gate") else nseg
    in_specs = [pl.BlockSpec((tm, d), lambda i: (i, 0))]
    args = [x]
    for s in range(nseg):
        in_specs.append(pl.BlockSpec((d, SEG), lambda i, s=s: (0, seg0 + s)))
        args.append(w)
    for s in range(nseg):
        in_specs.append(pl.BlockSpec((1, SEG), lambda i, s=s: (0, seg0 + s)))
        args.append(b)
    if tables is not None:
        cos, sin = tables
        nrep = cos.shape[0] // tm
        for t in (cos, sin):
            in_specs.append(pl.BlockSpec((tm, SEG), lambda i: (i % nrep, 0)))
            args.append(t)
    body = functools.partial(_proj_body, kinds=kinds, dh=dh, scale=dh ** -0.5)
    return pl.pallas_call(
        body,
        grid=(n // tm,),
        in_specs=in_specs,
        out_specs=pl.BlockSpec((tm, n_out * SEG), lambda i: (i, 0)),
        out_shape=jax.ShapeDtypeStruct((n, n_out * SEG), out_dtype),
        compiler_params=_cparams(("parallel",)),
        name=name,
    )(*args)


def _rope_angles(pos, dh):
    inv_freq = ROPE_THETA ** (-jnp.arange(0, dh, 2, dtype=F32) / dh)
    ang = pos.astype(F32)[:, None] * inv_freq[None, :]
    return jnp.concatenate([ang, ang], -1)


def _rope_tables(pos, dh):
    ang = _rope_angles(pos, dh)
    reps = SEG // dh
    return jnp.tile(jnp.cos(ang), (1, reps)), jnp.tile(jnp.sin(ang), (1, reps))


def _proj_kt_body(x_ref, wt_ref, b_ref, cos_ref, sin_ref, o_ref, *, dh):
    tm = x_ref.shape[0]
    y = _dot_nt(wt_ref[...], x_ref[...]) + b_ref[...]
    y3 = y.reshape(SEG // dh, dh, tm)
    rot = jnp.concatenate([-y3[:, dh // 2:], y3[:, :dh // 2]], axis=1)
    out = y3 * cos_ref[...][None] + rot * sin_ref[...][None]
    o_ref[...] = out.reshape(SEG, tm)


def _proj_kt(x, wt, bcol, pos, dh, bsz, t, tm, name):
    d = x.shape[1]
    ang = _rope_angles(pos, dh).T
    nt = t // tm
    return pl.pallas_call(
        functools.partial(_proj_kt_body, dh=dh),
        grid=(bsz, nt),
        in_specs=[
            pl.BlockSpec((tm, d), lambda bi, i: (bi * nt + i, 0)),
            pl.BlockSpec((SEG, d), lambda bi, i: (0, 0)),
            pl.BlockSpec((SEG, 1), lambda bi, i: (0, 0)),
            pl.BlockSpec((dh, tm), lambda bi, i: (0, i)),
            pl.BlockSpec((dh, tm), lambda bi, i: (0, i)),
        ],
        out_specs=pl.BlockSpec((None, SEG, tm), lambda bi, i: (bi, 0, i)),
        out_shape=jax.ShapeDtypeStruct((bsz, SEG, t), F32),
        compiler_params=_cparams(("parallel", "parallel")),
        name=name,
    )(x, wt, bcol, jnp.cos(ang), jnp.sin(ang))


def _conv_finish(acc, g_ref, b_ref, gate):
    mu = jnp.mean(acc, -1, keepdims=True)
    cen = acc - mu
    var = jnp.mean(cen * cen, -1, keepdims=True)
    y = cen * lax.rsqrt(var + LN_EPS) * g_ref[...] + b_ref[...]
    return _silu(y) * gate


def _conv_prompt_body(cur_ref, halo_ref, gate_ref, w_ref, bdw_ref, g_ref, b_ref, o_ref, xp_ref, xs_ref,
                      *, tt, hb, rb):
    i = pl.program_id(1)
    halo = halo_ref[...]
    xp_ref[0:hb, :] = jnp.where(i == 0, 0.0, halo)
    xp_ref[hb:hb + tt, :] = cur_ref[...]
    span = xs_ref.shape[1]
    for s in range(1, 8):
        xs_ref[s - 1] = xp_ref[s:s + span, :]
    off = hb - (CONV_W - 1)
    for r0 in range(0, tt, rb):
        acc = jnp.zeros((rb // 8, 8, SEG), F32)
        for j in range(CONV_W):
            s, base = (off + j) % 8, (off + j) // 8 * 8 + r0
            x = xp_ref[base:base + rb, :] if s == 0 else xs_ref[s - 1, base:base + rb, :]
            acc = acc + w_ref[j][None] * x.reshape(rb // 8, 8, SEG)
        rows = slice(r0, r0 + rb)
        acc = acc.reshape(rb, SEG) + bdw_ref[...]
        o_ref[rows, :] = _conv_finish(acc, g_ref, b_ref, gate_ref[rows, :]).astype(o_ref.dtype)


def _conv_prompt(glu, gate, w_dw, b_dw, g, b, bsz, t):
    tt, hb, rb = 256, 32, 32
    glu3 = glu.reshape(bsz, t, SEG)
    gate3 = gate.reshape(bsz, t, SEG)
    w_dw = jnp.broadcast_to(w_dw[:, None, :], (CONV_W, 8, SEG))
    vec = lambda: pl.BlockSpec((1, SEG), lambda bi, i: (0, 0))
    out = pl.pallas_call(
        functools.partial(_conv_prompt_body, tt=tt, hb=hb, rb=rb),
        grid=(bsz, t // tt),
        in_specs=[
            pl.BlockSpec((None, tt, SEG), lambda bi, i: (bi, i, 0)),
            pl.BlockSpec((None, hb, SEG), lambda bi, i: (bi, jnp.maximum(i * (tt // hb) - 1, 0), 0)),
            pl.BlockSpec((None, tt, SEG), lambda bi, i: (bi, i, 0)),
            pl.BlockSpec((CONV_W, 8, SEG), lambda bi, i: (0, 0, 0)),
            vec(), vec(), vec(),
        ],
        out_specs=pl.BlockSpec((None, tt, SEG), lambda bi, i: (bi, i, 0)),
        out_shape=jax.ShapeDtypeStruct((bsz, t, SEG), BF16),
        scratch_shapes=[pltpu.VMEM((hb + tt, SEG), F32), pltpu.VMEM((7, hb + tt - 8, SEG), F32)],
        compiler_params=_cparams(("parallel", "parallel")),
        name="conv_prompt",
    )(glu3, glu3, gate3, w_dw, b_dw, g, b)
    return out.reshape(bsz * t, SEG)


def _conv_sample_body(xp_ref, gate_ref, w_ref, bdw_ref, g_ref, b_ref, o_ref, *, ts):
    acc = jnp.zeros((ts, SEG), F32) + bdw_ref[...]
    for j in range(CONV_W):
        acc = acc + w_ref[j:j + 1, :] * xp_ref[j:j + ts, :]
    o_ref[...] = _conv_finish(acc, g_ref, b_ref, gate_ref[...]).astype(o_ref.dtype)


def _conv_sample(xpad, gate, w_dw, b_dw, g, b):
    bsz, rows, _ = xpad.shape
    ts = rows - (CONV_W - 1)
    gate3 = gate.reshape(bsz, ts, SEG)
    vec = lambda: pl.BlockSpec((1, SEG), lambda bi: (0, 0))
    out = pl.pallas_call(
        functools.partial(_conv_sample_body, ts=ts),
        grid=(bsz,),
        in_specs=[
            pl.BlockSpec((None, rows, SEG), lambda bi: (bi, 0, 0)),
            pl.BlockSpec((None, ts, SEG), lambda bi: (bi, 0, 0)),
            pl.BlockSpec((CONV_W, SEG), lambda bi: (0, 0)),
            vec(), vec(), vec(),
        ],
        out_specs=pl.BlockSpec((None, ts, SEG), lambda bi: (bi, 0, 0)),
        out_shape=jax.ShapeDtypeStruct((bsz, ts, SEG), BF16),
        compiler_params=_cparams(("parallel",)),
        name="conv_sample",
    )(xpad, gate3, w_dw, b_dw, g, b)
    return out.reshape(bsz * ts, SEG)


def _diff_lambda(lq1_ref, lk1_ref, lq2_ref, lk2_ref, lam_init):
    e1 = jnp.exp(jnp.sum(lq1_ref[...] * lk1_ref[...], keepdims=True))
    e2 = jnp.exp(jnp.sum(lq2_ref[...] * lk2_ref[...], keepdims=True))
    return e1 - e2 + lam_init


def _sub_norm(att, g_ref, lam_init):
    ms = jnp.mean(att * att, -1, keepdims=True)
    return att * lax.rsqrt(ms + LN_EPS) * g_ref[...] * (1.0 - lam_init)


def _diff_prompt_tile(i, q_ref, kt_ref, v_ref, gate_ref, lam, g_ref, o_ref, s_ref, side_work, *, tq, ck, lam_init):
    t = v_ref.shape[0]
    q = q_ref[...]
    lane = lax.broadcasted_iota(jnp.int32, q.shape, 1)
    zero = jnp.zeros_like(q)
    q2 = jnp.concatenate([jnp.where(lane < 64, q, zero), jnp.where(lane >= 64, q, zero)], axis=0)
    row = lax.broadcasted_iota(jnp.int32, (2 * tq, 1), 0)
    qpos = i * tq + jnp.where(row >= tq, row - tq, row)
    tiles_per_chunk = ck // tq

    def attend(nchunk):
        side_work()
        mx = jnp.full((2 * tq, 1), NEG, F32)
        for j in range(nchunk):
            cols = slice(j * ck, (j + 1) * ck)
            s = _dot(q2, kt_ref[:, cols].astype(BF16))
            if j == nchunk - 1:
                kpos = j * ck + lax.broadcasted_iota(jnp.int32, (1, ck), 1)
                s = jnp.where(kpos <= qpos, s, NEG)
            s_ref[:, cols] = s
            mx = jnp.maximum(mx, jnp.max(s, -1, keepdims=True))
        ones = jnp.ones((ck, LANES), BF16)
        l = jnp.zeros((2 * tq, LANES), F32)
        acc = jnp.zeros((2 * tq, LANES), F32)
        for j in range(nchunk):
            cols = slice(j * ck, (j + 1) * ck)
            p = jnp.exp(s_ref[:, cols] - mx).astype(BF16)
            l = l + _dot(p, ones)
            acc = acc + _dot(p, v_ref[cols, :].astype(BF16))
        o = acc / l
        att = o[:tq] - lam * o[tq:]
        o_ref[...] = (_sub_norm(att, g_ref, lam_init) * gate_ref[...]).astype(o_ref.dtype)

    for c in range(t // ck):
        pl.when(i // tiles_per_chunk == c)(functools.partial(attend, c + 1))


def _diff_attention_body(pt_ref, q_ref, *refs, npg, n_new, nmap, tq, ck, lam_init):
    k_refs = refs[:npg]
    v_refs = refs[npg:2 * npg]
    (knew_ref, vnew_ref, gate_ref, qp_ref, kt_ref, vp_ref, gatep_ref, lq1, lk1, lq2, lk2, g_ref,
     o_ref, op_ref, qbd_ref, m_ref, l_ref, acc_ref, kpad_ref, vpad_ref, s_ref) = refs[2 * npg:]
    c = pl.program_id(1)
    nh = nmap // 2
    dk = SEG // nmap
    rows_h = 2 * n_new
    lam = _diff_lambda(lq1, lk1, lq2, lk2, lam_init)

    @pl.when(c == 0)
    def _():
        qt = jnp.tile(q_ref[...].astype(F32), (nmap, 1))
        rmap = lax.broadcasted_iota(jnp.int32, qt.shape, 0) // n_new
        cmap = lax.broadcasted_iota(jnp.int32, qt.shape, 1) // dk
        qbd_ref[...] = jnp.where(rmap == cmap, qt, 0.0).astype(BF16)
        m_ref[...] = jnp.full(m_ref.shape, NEG, F32)
        l_ref[...] = jnp.zeros(l_ref.shape, F32)
        acc_ref[...] = jnp.zeros(acc_ref.shape, F32)

    def update(s, values_of_head):
        m_prev = m_ref[...]
        m_new = jnp.maximum(m_prev, jnp.max(s, -1, keepdims=True))
        alpha = jnp.exp(m_prev - m_new)
        p = jnp.exp(s - m_new)
        l_ref[...] = alpha * l_ref[...] + jnp.sum(p, -1, keepdims=True)
        m_ref[...] = m_new
        pb = p.astype(BF16)
        for h in range(nh):
            rows = slice(h * rows_h, (h + 1) * rows_h)
            acc_ref[rows, :] = alpha[rows] * acc_ref[rows, :] + _dot(pb[rows, :], values_of_head(h))

    def cache_chunk():
        qbd = qbd_ref[...]
        s = jnp.concatenate([_dot(qbd, k_refs[r][...].astype(BF16)) for r in range(npg)], axis=1)
        update(s, lambda h: jnp.concatenate(
            [v_refs[r][pl.ds(h, PAGE, stride=nh), :].astype(BF16) for r in range(npg)], axis=0))

    _diff_prompt_tile(c, qp_ref, kt_ref, vp_ref, gatep_ref, lam, g_ref, op_ref, s_ref, cache_chunk,
                      tq=tq, ck=ck, lam_init=lam_init)

    @pl.when(c == pl.num_programs(1) - 1)
    def _():
        qbd = qbd_ref[...]
        kpad_ref[...] = jnp.zeros(kpad_ref.shape, F32)
        vpad_ref[...] = jnp.zeros(vpad_ref.shape, F32)
        kpad_ref[0:n_new, :] = knew_ref[...]
        vpad_ref[0:n_new, :] = vnew_ref[...]
        s_new = _dot_nt(qbd, kpad_ref[...].astype(BF16))
        tnew = lax.broadcasted_iota(jnp.int32, s_new.shape, 1)
        qidx = lax.broadcasted_iota(jnp.int32, s_new.shape, 0) & (n_new - 1)
        s_new = jnp.where(tnew <= qidx, s_new, NEG)
        update(s_new, lambda h: vpad_ref[:, h * LANES:(h + 1) * LANES].astype(BF16))
        o = acc_ref[...] / l_ref[...]
        for h in range(nh):
            o1 = o[(2 * h) * n_new:(2 * h + 1) * n_new]
            o2 = o[(2 * h + 1) * n_new:(2 * h + 2) * n_new]
            att = _sub_norm(o1 - lam * o2, g_ref, lam_init)
            cols = slice(h * LANES, (h + 1) * LANES)
            o_ref[:, cols] = (att * gate_ref[:, cols]).astype(o_ref.dtype)


def _diff_attention(q_s, k_new, v_new, gate_s, cache_kt, cache_v, page_table,
                    q_p, kt_p, v_p, gate_p, lam_refs, subln_g, lam_init, n_new, bp, tp):
    nb, n_pages = page_table.shape
    nmap = 16
    npg = 8
    tq, ck = 256, 512
    nh = SEG // LANES
    nchunk = n_pages // npg
    assert bp * nh == nb and tp // tq == nchunk, "one prompt tile per decode chunk"
    r3 = lambda a: a.reshape(nb, n_new, SEG)
    p3 = lambda a: a.reshape(bp, tp, SEG)
    pt = page_table.reshape(-1)
    small = lambda w: pl.BlockSpec((1, w), lambda b, c, pt: (0, 0))
    tok = lambda: pl.BlockSpec((None, n_new, SEG), lambda b, c, pt: (b, 0, 0))
    ptile = lambda: pl.BlockSpec((None, tq, LANES), lambda b, c, pt: (b // nh, c, b % nh))

    def page_spec(r):
        return pl.BlockSpec((None, SEG, LANES), lambda b, c, pt, r=r: (pt[b * n_pages + c * npg + r], 0, 0))

    in_specs = [tok()]
    in_specs += [page_spec(r) for r in range(npg)]
    in_specs += [page_spec(r) for r in range(npg)]
    in_specs += [tok(), tok(), tok(),
                 ptile(),
                 pl.BlockSpec((None, LANES, tp), lambda b, c, pt: (b // nh, b % nh, 0)),
                 pl.BlockSpec((None, tp, LANES), lambda b, c, pt: (b // nh, 0, b % nh)),
                 ptile(),
                 small(64), small(64), small(64), small(64), small(LANES)]
    o_s, o_p = pl.pallas_call(
        functools.partial(_diff_attention_body, npg=npg, n_new=n_new, nmap=nmap, tq=tq, ck=ck, lam_init=lam_init),
        grid_spec=pltpu.PrefetchScalarGridSpec(
            num_scalar_prefetch=1,
            grid=(nb, nchunk),
            in_specs=in_specs,
            out_specs=[tok(), ptile()],
            scratch_shapes=[pltpu.VMEM((nmap * n_new, SEG), BF16),
                            pltpu.VMEM((nmap * n_new, 1), F32), pltpu.VMEM((nmap * n_new, 1), F32),
                            pltpu.VMEM((nmap * n_new, LANES), F32),
                            pltpu.VMEM((PAGE, SEG), F32), pltpu.VMEM((PAGE, SEG), F32),
                            pltpu.VMEM((2 * tq, tp), F32)],
        ),
        out_shape=[jax.ShapeDtypeStruct((nb, n_new, SEG), BF16), jax.ShapeDtypeStruct((bp, tp, SEG), BF16)],
        compiler_params=_cparams(("parallel", "arbitrary")),
        name="diff_attention",
    )(pt, r3(q_s), *([cache_kt] * npg), *([cache_v] * npg), r3(k_new), r3(v_new), r3(gate_s),
      p3(q_p), kt_p, p3(v_p), p3(gate_p), *lam_refs, subln_g)
    return o_p.reshape(bp * tp, SEG), o_s.reshape(nb * n_new, SEG)


def _out_ln_body(*refs, nmix, alpha):
    mix_refs = refs[:nmix]
    w_ref, x_ref, g_ref, b_ref, o_ref, ob_ref = refs[nmix:]
    f = _dot(mix_refs[0][...], w_ref[0:SEG, :])
    for s in range(1, nmix):
        f = f + _dot(mix_refs[s][...], w_ref[s * SEG:(s + 1) * SEG, :])
    z = alpha * x_ref[...] + f
    mu = jnp.mean(z, -1, keepdims=True)
    cen = z - mu
    var = jnp.mean(cen * cen, -1, keepdims=True)
    y = cen * lax.rsqrt(var + LN_EPS) * g_ref[...] + b_ref[...]
    o_ref[...] = y
    ob_ref[...] = y.astype(BF16)


def _out_ln(mixes, w, x, g, b, alpha, tm, name):
    n, d = x.shape
    nmix = len(mixes)
    in_specs = [pl.BlockSpec((tm, SEG), lambda i: (i, 0)) for _ in mixes]
    in_specs += [
        pl.BlockSpec((nmix * SEG, d), lambda i: (0, 0)),
        pl.BlockSpec((tm, d), lambda i: (i, 0)),
        pl.BlockSpec((1, d), lambda i: (0, 0)),
        pl.BlockSpec((1, d), lambda i: (0, 0)),
    ]
    return pl.pallas_call(
        functools.partial(_out_ln_body, nmix=nmix, alpha=alpha),
        grid=(n // tm,),
        in_specs=in_specs,
        out_specs=[pl.BlockSpec((tm, d), lambda i: (i, 0)), pl.BlockSpec((tm, d), lambda i: (i, 0))],
        out_shape=[jax.ShapeDtypeStruct((n, d), F32), jax.ShapeDtypeStruct((n, d), BF16)],
        compiler_params=_cparams(("parallel",)),
        name=name,
    )(*mixes, w, x, g, b)


def _dilated_prompt_group(q_ref, k_ref, v_ref, og_ref, ls_ref, win, dil, blk):
    t = q_ref.shape[0]
    assert win // dil == blk

    def rows(ref, start, n):
        if dil == 1:
            return ref[start:start + n, :]
        return ref[pl.ds(start, n, stride=dil), :]

    def put(ref, start, n, val):
        if dil == 1:
            ref[start:start + n, :] = val
        else:
            ref[pl.ds(start, n, stride=dil), :] = val

    for r in range(dil):
        for i in range(t // dil // blk):
            base = r + dil * blk * i
            q = rows(q_ref, base, blk).astype(BF16)
            kstart, nk = (r, blk) if i == 0 else (base - dil * blk, 2 * blk)
            kb = rows(k_ref, kstart, nk).astype(BF16)
            vb = rows(v_ref, kstart, nk).astype(BF16)
            s = _dot_nt(q, kb)
            a = lax.broadcasted_iota(jnp.int32, s.shape, 0)
            b = lax.broadcasted_iota(jnp.int32, s.shape, 1)
            ok = (b <= a) if i == 0 else ((b >= a) & (b <= a + blk))
            s = jnp.where(ok, s, NEG)
            m = jnp.max(s, -1, keepdims=True)
            p = jnp.exp(s - m)
            l = jnp.sum(p, -1, keepdims=True)
            put(og_ref, base, blk, _dot(p.astype(BF16), vb) / l)
            put(ls_ref, base, blk, jnp.broadcast_to(m + jnp.log(l), (blk, LANES)))


def _dilated_prompt_merge(ogs, lss, gate_ref, o_ref):
    la, lb, lc = lss[0][...], lss[1][...], lss[2][...]
    top = jnp.maximum(jnp.maximum(la, lb), lc)
    wa, wb, wc = jnp.exp(la - top), jnp.exp(lb - top), jnp.exp(lc - top)
    o = (ogs[0][...] * wa + ogs[1][...] * wb + ogs[2][...] * wc) / (wa + wb + wc)
    o_ref[...] = (o * gate_ref[...]).astype(o_ref.dtype)


def _diag_mask(g, per):
    sub = lax.broadcasted_iota(jnp.int32, (g, LANES), 0)
    lane = lax.broadcasted_iota(jnp.int32, (g, LANES), 1)
    return sub == lane // per


def _diag_row(x, diag):
    return jnp.sum(jnp.where(diag, x, 0.0), axis=0, keepdims=True)


def _dilated_sample_body(qc_ref, buf_ref, nxt_ref, new_ref, *rest, rows, n_new, nh, dil, fused):
    if fused:
        (q0, q1, q2, k0, v0, k1, v1, k2, v2, gatep_ref, o_ref, lse_ref, nbuf_hbm, op_ref,
         m_ref, l_ref, acc_ref, sem, og0, og1, og2, ls0, ls1, ls2) = rest
    else:
        o_ref, lse_ref, nbuf_hbm, m_ref, l_ref, acc_ref, sem = rest
    b = pl.program_id(0)
    c = pl.program_id(1)
    last = c == pl.num_programs(1) - 1
    diag = _diag_mask(nh, n_new)
    qc = qc_ref[...]

    body_copy = pltpu.make_async_copy(buf_ref.at[0, pl.ds(n_new, rows - n_new)],
                                      nbuf_hbm.at[b, pl.ds(c * rows, rows - n_new)], sem.at[0])
    tail_dst = nbuf_hbm.at[b, pl.ds(c * rows + rows - n_new, n_new)]
    tail_new = pltpu.make_async_copy(new_ref.at[0], tail_dst, sem.at[1])
    tail_nxt = pltpu.make_async_copy(nxt_ref.at[0], tail_dst, sem.at[1])
    body_copy.start()
    pl.when(last)(tail_new.start)
    pl.when(jnp.logical_not(last))(tail_nxt.start)

    @pl.when(c == 0)
    def _():
        m_ref[...] = jnp.where(diag, -jnp.inf, 0.0)
        l_ref[...] = jnp.zeros(l_ref.shape, F32)
        acc_ref[...] = jnp.zeros(acc_ref.shape, F32)

    def update(k3, v3, mask3):
        n_t = k3.shape[0]
        n = n_t * nh
        s3 = _dot(k3.reshape(n, LANES).astype(BF16), qc).reshape(n_t, nh, LANES)
        s3 = jnp.where(mask3, s3, -jnp.inf)
        m_old = m_ref[...]
        m_new = jnp.maximum(m_old, jnp.max(s3, axis=0))
        m_safe = jnp.where(m_new == -jnp.inf, 0.0, m_new)
        alpha = jnp.exp(m_old - m_safe)
        p3 = jnp.exp(s3 - m_safe[None])
        l_ref[...] = alpha * l_ref[...] + jnp.sum(p3, axis=0)
        m_ref[...] = m_new
        v2 = v3.reshape(n, LANES).astype(BF16)
        acc_ref[...] = acc_ref[...] * _diag_row(alpha, diag) + _dot_tn(v2, p3.reshape(n, LANES).astype(BF16))

    def buffer_chunk():
        k3, v3 = buf_ref[0, :, 0:nh, :], buf_ref[0, :, nh:2 * nh, :]
        idx = lax.broadcasted_iota(jnp.int32, (rows, nh, LANES), 0)
        if dil > n_new:
            keep = lambda a: a.reshape(rows // dil, dil, nh, LANES)[:, 0:n_new].reshape(rows // dil * n_new, nh, LANES)
            k3, v3, idx = keep(k3), keep(v3), idx[0:rows // dil * n_new]
            idx = (idx // n_new) * dil + (idx & (n_new - 1))
        rho = c * rows + idx
        tq = lax.broadcasted_iota(jnp.int32, rho.shape, 2) & (n_new - 1)
        seen = (rho >= tq) & (((rho - tq) & (dil - 1)) == 0)
        update(k3, v3, diag[None] & seen)

    if not fused:
        buffer_chunk()
    else:
        blk = C_WINDOWS[0] // C_DILATIONS[0]
        groups = ((q0, k0, v0, og0, ls0), (q1, k1, v1, og1, ls1), (q2, k2, v2, og2, ls2))

        def with_group(g):
            buffer_chunk()
            _dilated_prompt_group(*groups[g], C_WINDOWS[g], C_DILATIONS[g], blk)

        def with_merge():
            buffer_chunk()
            _dilated_prompt_merge((og0, og1, og2), (ls0, ls1, ls2), gatep_ref, op_ref)

        for g in range(len(groups)):
            pl.when(c == g)(functools.partial(with_group, g))
        pl.when(c == len(groups))(with_merge)

    @pl.when(last)
    def _():
        tn = lax.broadcasted_iota(jnp.int32, (n_new, nh, LANES), 0)
        tl = lax.broadcasted_iota(jnp.int32, (n_new, nh, LANES), 2) & (n_new - 1)
        dist = tl - tn
        ok = (dist >= 0) & ((dist & (dil - 1)) == 0)
        update(new_ref[0, :, 0:nh, :], new_ref[0, :, nh:2 * nh, :], diag[None] & ok)
        lane = lax.broadcasted_iota(jnp.int32, (1, LANES), 1)
        used = lane < nh * n_new
        l_row = jnp.where(used, _diag_row(l_ref[...], diag), 1.0)
        m_row = jnp.where(used, _diag_row(m_ref[...], diag), 0.0)
        o = (acc_ref[...] / l_row).T
        lse = jnp.broadcast_to(m_row + jnp.log(l_row), (LANES, LANES)).T
        for h in range(nh):
            cols = slice(h * LANES, (h + 1) * LANES)
            o_ref[:, cols] = o[h * n_new:(h + 1) * n_new]
            lse_ref[:, cols] = lse[h * n_new:(h + 1) * n_new]

    body_copy.wait()
    pl.when(last)(tail_new.wait)
    pl.when(jnp.logical_not(last))(tail_nxt.wait)


def _dilated_sample(q, buf4, new4, dil, prompt=None):
    nb, lb = buf4.shape[0], buf4.shape[1]
    n_new = new4.shape[1]
    nh = buf4.shape[2] // 2
    rows = min(lb, 512)
    nch = lb // rows
    qc = q.reshape(nb, n_new, nh, LANES).transpose(0, 3, 2, 1).reshape(nb, LANES, nh * n_new)
    qc = jnp.pad(qc, ((0, 0), (0, 0), (0, LANES - nh * n_new)))
    ospec = lambda: pl.BlockSpec((None, n_new, SEG), lambda b, c: (b, 0, 0))
    per = rows // n_new
    in_specs = [
        pl.BlockSpec((None, LANES, LANES), lambda b, c: (b, 0, 0)),
        pl.BlockSpec((1, rows, 2 * nh, LANES), lambda b, c: (b, c, 0, 0)),
        pl.BlockSpec((1, n_new, 2 * nh, LANES),
                     lambda b, c: (b, jnp.minimum((c + 1) * per, nch * per - 1), 0, 0)),
        pl.BlockSpec((1, n_new, 2 * nh, LANES), lambda b, c: (b, 0, 0, 0)),
    ]
    args = [qc, buf4, buf4, new4]
    out_specs = [ospec(), ospec(), pl.BlockSpec(memory_space=pl.ANY)]
    out_shape = [jax.ShapeDtypeStruct((nb, n_new, SEG), F32), jax.ShapeDtypeStruct((nb, n_new, SEG), F32),
                 jax.ShapeDtypeStruct(buf4.shape, F32)]
    scratch = [pltpu.VMEM((nh, LANES), F32), pltpu.VMEM((nh, LANES), F32),
               pltpu.VMEM((LANES, LANES), F32), pltpu.SemaphoreType.DMA((2,))]
    if prompt is not None:
        qs, kvs, gate, bp, tp = prompt
        assert bp * nh == nb and nch == len(qs) + 1, "one prompt (batch, head) per sequence, one group per chunk"
        hspec = lambda off=0: pl.BlockSpec((None, tp, LANES), lambda b, c: (b // nh, 0, off + b % nh))
        kv3 = [kv.reshape(bp, tp, 2 * SEG) for kv in kvs]
        in_specs += [hspec(), hspec(), hspec(), hspec(), hspec(nh), hspec(), hspec(nh), hspec(), hspec(nh), hspec()]
        args += [a.reshape(bp, tp, SEG) for a in qs]
        args += [kv3[0], kv3[0], kv3[1], kv3[1], kv3[2], kv3[2], gate.reshape(bp, tp, SEG)]
        out_specs.append(hspec())
        out_shape.append(jax.ShapeDtypeStruct((bp, tp, SEG), BF16))
        scratch += [pltpu.VMEM((tp, LANES), F32) for _ in range(6)]
    return pl.pallas_call(
        functools.partial(_dilated_sample_body, rows=rows, n_new=n_new, nh=nh, dil=dil, fused=prompt is not None),
        grid=(nb, nch),
        in_specs=in_specs,
        out_specs=out_specs,
        out_shape=out_shape,
        scratch_shapes=scratch,
        compiler_params=_cparams(("parallel", "arbitrary")),
        name=f"dilated_sample_d{dil}",
    )(*args)


def _merge_sample_body(o0, o1, o2, l0, l1, l2, gate_ref, o_ref):
    a, b, c = l0[...], l1[...], l2[...]
    top = jnp.maximum(jnp.maximum(a, b), c)
    wa, wb, wc = jnp.exp(a - top), jnp.exp(b - top), jnp.exp(c - top)
    o = (o0[...] * wa + o1[...] * wb + o2[...] * wc) / (wa + wb + wc)
    o_ref[...] = (o * gate_ref[...]).astype(o_ref.dtype)


def _merge_sample(outs, lses, gate):
    n = gate.shape[0]
    spec = lambda: pl.BlockSpec((n, SEG), lambda i: (0, 0))
    flat = lambda a: a.reshape(n, SEG)
    return pl.pallas_call(
        _merge_sample_body,
        grid=(1,),
        in_specs=[spec() for _ in range(7)],
        out_specs=spec(),
        out_shape=jax.ShapeDtypeStruct((n, SEG), BF16),
        compiler_params=_cparams(("arbitrary",)),
        name="merge_sample",
    )(*[flat(o) for o in outs], *[flat(l) for l in lses], gate)


def kernel(x_prompt, x_sample, cache_kb, cache_vb, state_conv, state_kv_c0, state_kv_c1, state_kv_c2,
           page_table, w_in_ab, b_in_ab, w_dw, b_dw, ln_a_g, ln_a_b, lam_q1, lam_k1, lam_q2, lam_k2,
           subln_g, w_out_ab, w_in_c, b_in_c, w_out_c, post_ln_g, post_ln_b):
    bp, tp, d = x_prompt.shape
    bs, ts, _ = x_sample.shape
    n_p, n_s = bp * tp, bs * ts
    n_phys = cache_kb.shape[1]
    past_len = page_table.shape[1] * PAGE
    depth = post_ln_g.shape[0]
    alpha = (2 * depth) ** 0.25
    states_c = (state_kv_c0, state_kv_c1, state_kv_c2)
    tm_p, tm_s = 512, n_s
    row = lambda a: a.reshape(1, -1)

    pos_p = jnp.arange(tp)
    pos_s = jnp.tile(past_len + jnp.arange(ts), bs)
    tabs = {(64, "p"): _rope_tables(pos_p, 64), (64, "s"): _rope_tables(pos_s, 64),
            (128, "p"): _rope_tables(pos_p, 128), (128, "s"): _rope_tables(pos_s, 128)}

    xp32 = x_prompt.reshape(n_p, d)
    xs32 = x_sample.reshape(n_s, d)
    xp16 = xp32.astype(BF16)
    xs16 = xs32.astype(BF16)

    lam_init = 0.8 - 0.6 * math.exp(-0.3 * 0)
    w_in = w_in_ab[0].astype(BF16)
    b_in = row(b_in_ab[0])
    w_out = w_out_ab[0].astype(BF16)
    lam_refs = [row(lam_q1[0]), row(lam_k1[0]), row(lam_q2[0]), row(lam_k2[0])]
    sub_g = row(subln_g[0])
    conv_args = (w_dw[0], row(b_dw[0]), row(ln_a_g[0]), row(ln_a_b[0]))
    kseg = 4

    def ab_project(x16, tm, grp):
        t64 = tabs[(64, grp)]
        pr = functools.partial(_proj, x16, w_in, b_in, tm=tm, dh=64)
        glu = pr(seg0=0, kinds=("glu_val", "glu_gate"), out_dtype=F32, name="proj_glu_" + grp)
        g_a = pr(seg0=2, kinds=("silu",), out_dtype=F32, name="proj_ga_" + grp)
        q = pr(seg0=3, kinds=("rope_q",), out_dtype=BF16, tables=t64, name="proj_qb_" + grp)
        v = pr(seg0=5, kinds=("none",), out_dtype=F32, name="proj_vb_" + grp)
        g_b = pr(seg0=6, kinds=("silu",), out_dtype=F32, name="proj_gb_" + grp)
        return glu, g_a, q, v, g_b

    glu, g_a, q, v, g_b = ab_project(xp16, tm_p, "p")
    kt = _proj_kt(xp16, w_in[:, kseg * SEG:(kseg + 1) * SEG].T, b_in_ab[0, kseg * SEG:(kseg + 1) * SEG].reshape(SEG, 1),
                  pos_p, 64, bp, tp, tm_p, "proj_kbt_p")
    a_mix = _conv_prompt(glu, g_a, *conv_args, bp, tp)
    kb_p = kt.reshape(1, bp, 16, 64, tp).transpose(0, 1, 4, 2, 3)
    vb_p = v.reshape(1, bp, tp, 8, 128)
    conv_p = glu.reshape(bp, tp, SEG)[None, :, tp - (CONV_W - 1):]

    glu, g_a, q_s, v_s, gb_s = ab_project(xs16, tm_s, "s")
    k_s = _proj(xs16, w_in, b_in, seg0=kseg, kinds=("rope",), out_dtype=F32, tm=tm_s, tables=tabs[(64, "s")],
                dh=64, name="proj_kb_s")
    xpad = jnp.concatenate([state_conv[0], glu.reshape(bs, ts, SEG)], axis=1)
    a_mix_s = _conv_sample(xpad, g_a, *conv_args)
    cache_kt = cache_kb[0].transpose(0, 2, 3, 1).reshape(n_phys, SEG, PAGE)
    cache_v = cache_vb[0].reshape(n_phys, PAGE * 8, LANES)
    b_mix, b_mix_s = _diff_attention(q_s, k_s, v_s, gb_s, cache_kt, cache_v, page_table,
                                     q, kt, v, g_b, lam_refs, sub_g, lam_init, ts, bp, tp)
    xp32, xp16 = _out_ln([a_mix, b_mix], w_out, xp32, row(post_ln_g[0]), row(post_ln_b[0]), alpha, tm_p, "out_ab_p")
    xs32, xs16 = _out_ln([a_mix_s, b_mix_s], w_out, xs32, row(post_ln_g[0]), row(post_ln_b[0]), alpha, tm_s,
                         "out_ab_s")
    kb_s = k_s.reshape(1, bs, ts, 16, 64)
    vb_s = v_s.reshape(1, bs, ts, 8, 128)
    conv_s = xpad[None, :, ts:]

    w_in = w_in_c[0].astype(BF16)
    b_in = row(b_in_c[0])
    w_out = w_out_c[0].astype(BF16)
    ngrp = len(C_WINDOWS)

    def c_project(x16, tm, grp, q_dtype):
        t128 = tabs[(128, grp)]
        pr = functools.partial(_proj, x16, w_in, b_in, tm=tm, dh=128)
        qs = [pr(seg0=3 * g, kinds=("rope_q",), out_dtype=q_dtype, tables=t128, name=f"proj_qc{g}_" + grp)
              for g in range(ngrp)]
        kvs = [pr(seg0=3 * g + 1, kinds=("rope", "none"), out_dtype=F32, tables=t128, name=f"proj_kvc{g}_" + grp)
               for g in range(ngrp)]
        gate = pr(seg0=3 * ngrp, kinds=("silu",), out_dtype=F32, name="proj_gc_" + grp)
        return qs, kvs, gate

    qs_p, kvs_p, gate_p = c_project(xp16, tm_p, "p", F32)
    kvc_p = [kv.reshape(bp, tp, 2, 8, 128)[None, :, tp - min(w, tp):] for kv, w in zip(kvs_p, C_WINDOWS)]

    qs, kvs, gate = c_project(xs16, tm_s, "s", BF16)
    outs, lses, kvc_s = [], [], []
    for g in range(ngrp):
        buf = states_c[g][0]
        lb = buf.shape[1]
        prompt = (qs_p, kvs_p, gate_p, bp, tp) if g == ngrp - 1 else None
        res = _dilated_sample(qs[g], buf.reshape(bs, lb, 16, LANES), kvs[g].reshape(bs, ts, 16, LANES),
                              C_DILATIONS[g], prompt)
        outs.append(res[0])
        lses.append(res[1])
        kvc_s.append(res[2].reshape(1, bs, lb, 2, 8, 128))
    o_mix_p = res[3].reshape(n_p, SEG)
    y_p, _ = _out_ln([o_mix_p], w_out, xp32, row(post_ln_g[1]), row(post_ln_b[1]), alpha, tm_p, "out_c_p")
    o_mix = _merge_sample(outs, lses, gate)
    y_s, _ = _out_ln([o_mix], w_out, xs32, row(post_ln_g[1]), row(post_ln_b[1]), alpha, tm_s, "out_c_s")

    return (y_p.reshape(bp, tp, d), y_s.reshape(bs, ts, d),
            kb_p, vb_p, conv_p, kb_s, vb_s, conv_s,
            kvc_p[0], kvc_p[1], kvc_p[2], kvc_s[0], kvc_s[1], kvc_s[2])
```

```python
import functools
import math

import jax
import jax.numpy as jnp
from jax import lax
from jax.experimental import pallas as pl
from jax.experimental.pallas import tpu as pltpu

F32 = jnp.float32
BF16 = jnp.bfloat16

LANES = 128
SEG = 1024
LN_EPS = 1e-5
ROPE_THETA = 10000.0
CONV_W = 31
PAGE = 128
C_WINDOWS = (128, 512, 2048)
C_DILATIONS = (1, 4, 16)
NEG = -1e30
VMEM_LIMIT = 56 * 1024 * 1024


def _cparams(sem):
    return pltpu.CompilerParams(dimension_semantics=sem, vmem_limit_bytes=VMEM_LIMIT)


def _dot(a, b):
    return jnp.dot(a, b, preferred_element_type=F32)


def _dot_nt(a, b):
    return lax.dot_general(a, b, (((1,), (1,)), ((), ())), preferred_element_type=F32)


def _dot_tn(a, b):
    return lax.dot_general(a, b, (((0,), (0,)), ((), ())), preferred_element_type=F32)


def _silu(x):
    return x * jax.nn.sigmoid(x)


def _rope_rotate(y, dh):
    half = dh // 2
    lane = lax.broadcasted_iota(jnp.int32, y.shape, 1)
    first = (lane & (dh - 1)) < half
    fwd = pltpu.roll(y, SEG - half, axis=1)
    bwd = pltpu.roll(y, half, axis=1)
    return jnp.where(first, -fwd, bwd)


def _proj_body(*refs, kinds, dh, scale):
    nseg = len(kinds)
    x_ref = refs[0]
    w_refs = refs[1:1 + nseg]
    b_refs = refs[1 + nseg:1 + 2 * nseg]
    rest = refs[1 + 2 * nseg:]
    if "rope" in kinds or "rope_q" in kinds:
        cos_ref, sin_ref, o_ref = rest
    else:
        (o_ref,) = rest
    x = x_ref[...].astype(BF16)
    if kinds == ("glu_val", "glu_gate"):
        val = _dot(x, w_refs[0][...]) + b_refs[0][...]
        gate = _dot(x, w_refs[1][...]) + b_refs[1][...]
        o_ref[...] = (val * jax.nn.sigmoid(gate)).astype(o_ref.dtype)
        return
    for s, kind in enumerate(kinds):
        y = _dot(x, w_refs[s][...]) + b_refs[s][...]
        if kind in ("rope", "rope_q"):
            y = y * cos_ref[...] + _rope_rotate(y, dh) * sin_ref[...]
            if kind == "rope_q":
                y = y * scale
        elif kind == "silu":
            y = _silu(y)
        o_ref[:, s * SEG:(s + 1) * SEG] = y.astype(o_ref.dtype)


def _proj(x, w, b, seg0, kinds, out_dtype, tm, tables=None, dh=64, name="proj"):
    n, d = x.shape
    nseg = len(kinds)
    n_out = 1 if kinds == ("glu_val", "glu_gate") else nseg
    in_specs = [pl.BlockSpec((tm, d), lambda i: (i, 0))]
    args = [x]
    for s in range(nseg):
        in_specs.append(pl.BlockSpec((d, SEG), lambda i, s=s: (0, seg0 + s)))
        args.append(w)
    for s in range(nseg):
        in_specs.append(pl.BlockSpec((1, SEG), lambda i, s=s: (0, seg0 + s)))
        args.append(b)
    if tables is not None:
        cos, sin = tables
        nrep = cos.shape[0] // tm
        for t in (cos, sin):
            in_specs.append(pl.BlockSpec((tm, SEG), lambda i: (i % nrep, 0)))
            args.append(t)
    body = functools.partial(_proj_body, kinds=kinds, dh=dh, scale=dh ** -0.5)
    return pl.pallas_call(
        body,
        grid=(n // tm,),
        in_specs=in_specs,
        out_specs=pl.BlockSpec((tm, n_out * SEG), lambda i: (i, 0)),
        out_shape=jax.ShapeDtypeStruct((n, n_out * SEG), out_dtype),
        compiler_params=_cparams(("parallel",)),
        name=name,
    )(*args)


def _proj_multi_body(x_ref, w_ref, b_ref, cos_ref, sin_ref, o_ref, val_ref, *, kinds, dh, scale):
    j = pl.program_id(0)
    y = _dot(x_ref[...].astype(BF16), w_ref[...]) + b_ref[...]

    def finish(kind):
        if kind == "glu_val":
            val_ref[...] = y
            o_ref[...] = y
        elif kind == "glu_gate":
            o_ref[...] = val_ref[...] * jax.nn.sigmoid(y)
        elif kind in ("rope", "rope_q"):
            z = y * cos_ref[...] + _rope_rotate(y, dh) * sin_ref[...]
            o_ref[...] = z * scale if kind == "rope_q" else z
        elif kind == "silu":
            o_ref[...] = _silu(y)
        else:
            o_ref[...] = y

    for k, kind in enumerate(kinds):
        pl.when(j == k)(functools.partial(finish, kind))


def _proj_multi(x, w, b, kinds, tables, dh, name):
    n, d = x.shape
    nseg = len(kinds)
    cos, sin = tables
    const = lambda shape: pl.BlockSpec(shape, lambda j: (0, 0))
    return pl.pallas_call(
        functools.partial(_proj_multi_body, kinds=kinds, dh=dh, scale=dh ** -0.5),
        grid=(nseg,),
        in_specs=[const((n, d)),
                  pl.BlockSpec((d, SEG), lambda j: (0, j)),
                  pl.BlockSpec((1, SEG), lambda j: (0, j)),
                  const((n, SEG)), const((n, SEG))],
        out_specs=pl.BlockSpec((n, SEG), lambda j: (0, j)),
        out_shape=jax.ShapeDtypeStruct((n, nseg * SEG), F32),
        scratch_shapes=[pltpu.VMEM((n, SEG), F32)],
        compiler_params=_cparams(("arbitrary",)),
        name=name,
    )(x, w, b, cos, sin)


def _rope_angles(pos, dh):
    inv_freq = ROPE_THETA ** (-jnp.arange(0, dh, 2, dtype=F32) / dh)
    ang = pos.astype(F32)[:, None] * inv_freq[None, :]
    return jnp.concatenate([ang, ang], -1)


def _rope_tables(pos, dh):
    ang = _rope_angles(pos, dh)
    reps = SEG // dh
    return jnp.tile(jnp.cos(ang), (1, reps)), jnp.tile(jnp.sin(ang), (1, reps))


def _proj_kt_body(x_ref, wt_ref, b_ref, cos_ref, sin_ref, o_ref, *, dh):
    tm = x_ref.shape[0]
    y = _dot_nt(wt_ref[...], x_ref[...].astype(BF16)) + b_ref[...]
    y3 = y.reshape(SEG // dh, dh, tm)
    rot = jnp.concatenate([-y3[:, dh // 2:], y3[:, :dh // 2]], axis=1)
    out = y3 * cos_ref[...][None] + rot * sin_ref[...][None]
    o_ref[...] = out.reshape(SEG, tm)


def _proj_kt(x, wt, bcol, pos, dh, bsz, t, tm, name):
    d = x.shape[1]
    ang = _rope_angles(pos, dh).T
    nt = t // tm
    return pl.pallas_call(
        functools.partial(_proj_kt_body, dh=dh),
        grid=(bsz, nt),
        in_specs=[
            pl.BlockSpec((tm, d), lambda bi, i: (bi * nt + i, 0)),
            pl.BlockSpec((SEG, d), lambda bi, i: (0, 0)),
            pl.BlockSpec((SEG, 1), lambda bi, i: (0, 0)),
            pl.BlockSpec((dh, tm), lambda bi, i: (0, i)),
            pl.BlockSpec((dh, tm), lambda bi, i: (0, i)),
        ],
        out_specs=pl.BlockSpec((None, SEG, tm), lambda bi, i: (bi, 0, i)),
        out_shape=jax.ShapeDtypeStruct((bsz, SEG, t), F32),
        compiler_params=_cparams(("parallel", "parallel")),
        name=name,
    )(x, wt, bcol, jnp.cos(ang), jnp.sin(ang))


def _conv_finish(acc, g_ref, b_ref, gate):
    mu = jnp.mean(acc, -1, keepdims=True)
    cen = acc - mu
    var = jnp.mean(cen * cen, -1, keepdims=True)
    y = cen * lax.rsqrt(var + LN_EPS) * g_ref[...] + b_ref[...]
    return _silu(y) * gate


def _conv_prompt_body(cur_ref, halo_ref, gate_ref, w_ref, bdw_ref, g_ref, b_ref, o_ref, xp_ref, xs_ref,
                      *, tt, hb, rb):
    i = pl.program_id(1)
    halo = halo_ref[...]
    xp_ref[0:hb, :] = jnp.where(i == 0, 0.0, halo)
    xp_ref[hb:hb + tt, :] = cur_ref[...]
    span = xs_ref.shape[1]
    for s in range(1, 8):
        xs_ref[s - 1] = xp_ref[s:s + span, :]
    off = hb - (CONV_W - 1)
    for r0 in range(0, tt, rb):
        acc = jnp.zeros((rb // 8, 8, SEG), F32)
        for j in range(CONV_W):
            s, base = (off + j) % 8, (off + j) // 8 * 8 + r0
            x = xp_ref[base:base + rb, :] if s == 0 else xs_ref[s - 1, base:base + rb, :]
            acc = acc + w_ref[j][None] * x.reshape(rb // 8, 8, SEG)
        rows = slice(r0, r0 + rb)
        acc = acc.reshape(rb, SEG) + bdw_ref[...]
        o_ref[rows, :] = _conv_finish(acc, g_ref, b_ref, gate_ref[rows, :]).astype(o_ref.dtype)


def _conv_prompt(glu, gate, w_dw, b_dw, g, b, bsz, t):
    tt, hb, rb = 256, 32, 32
    glu3 = glu.reshape(bsz, t, SEG)
    gate3 = gate.reshape(bsz, t, SEG)
    w_dw = jnp.broadcast_to(w_dw[:, None, :], (CONV_W, 8, SEG))
    vec = lambda: pl.BlockSpec((1, SEG), lambda bi, i: (0, 0))
    out = pl.pallas_call(
        functools.partial(_conv_prompt_body, tt=tt, hb=hb, rb=rb),
        grid=(bsz, t // tt),
        in_specs=[
            pl.BlockSpec((None, tt, SEG), lambda bi, i: (bi, i, 0)),
            pl.BlockSpec((None, hb, SEG), lambda bi, i: (bi, jnp.maximum(i * (tt // hb) - 1, 0), 0)),
            pl.BlockSpec((None, tt, SEG), lambda bi, i: (bi, i, 0)),
            pl.BlockSpec((CONV_W, 8, SEG), lambda bi, i: (0, 0, 0)),
            vec(), vec(), vec(),
        ],
        out_specs=pl.BlockSpec((None, tt, SEG), lambda bi, i: (bi, i, 0)),
        out_shape=jax.ShapeDtypeStruct((bsz, t, SEG), BF16),
        scratch_shapes=[pltpu.VMEM((hb + tt, SEG), F32), pltpu.VMEM((7, hb + tt - 8, SEG), F32)],
        compiler_params=_cparams(("parallel", "parallel")),
        name="conv_prompt",
    )(glu3, glu3, gate3, w_dw, b_dw, g, b)
    return out.reshape(bsz * t, SEG)


def _conv_sample_body(xp_ref, gate_ref, w_ref, bdw_ref, g_ref, b_ref, o_ref, *, ts):
    acc = jnp.zeros((ts, SEG), F32) + bdw_ref[...]
    for j in range(CONV_W):
        acc = acc + w_ref[j:j + 1, :] * xp_ref[j:j + ts, :]
    o_ref[...] = _conv_finish(acc, g_ref, b_ref, gate_ref[...]).astype(o_ref.dtype)


def _conv_sample(xpad, gate, w_dw, b_dw, g, b):
    bsz, rows, _ = xpad.shape
    ts = rows - (CONV_W - 1)
    gate3 = gate.reshape(bsz, ts, SEG)
    vec = lambda: pl.BlockSpec((1, SEG), lambda bi: (0, 0))
    out = pl.pallas_call(
        functools.partial(_conv_sample_body, ts=ts),
        grid=(bsz,),
        in_specs=[
            pl.BlockSpec((None, rows, SEG), lambda bi: (bi, 0, 0)),
            pl.BlockSpec((None, ts, SEG), lambda bi: (bi, 0, 0)),
            pl.BlockSpec((CONV_W, SEG), lambda bi: (0, 0)),
            vec(), vec(), vec(),
        ],
        out_specs=pl.BlockSpec((None, ts, SEG), lambda bi: (bi, 0, 0)),
        out_shape=jax.ShapeDtypeStruct((bsz, ts, SEG), BF16),
        compiler_params=_cparams(("parallel",)),
        name="conv_sample",
    )(xpad, gate3, w_dw, b_dw, g, b)
    return out.reshape(bsz * ts, SEG)


def _diff_lambda(lq1_ref, lk1_ref, lq2_ref, lk2_ref, lam_init):
    e1 = jnp.exp(jnp.sum(lq1_ref[...] * lk1_ref[...], keepdims=True))
    e2 = jnp.exp(jnp.sum(lq2_ref[...] * lk2_ref[...], keepdims=True))
    return e1 - e2 + lam_init


def _sub_norm(att, g_ref, lam_init):
    ms = jnp.mean(att * att, -1, keepdims=True)
    return att * lax.rsqrt(ms + LN_EPS) * g_ref[...] * (1.0 - lam_init)


def _diff_prompt_tile(i, q_ref, kt_ref, v_ref, gate_ref, lam, g_ref, o_ref, s_ref, side_work, *, tq, ck, lam_init):
    t = v_ref.shape[0]
    q = q_ref[...]
    lane = lax.broadcasted_iota(jnp.int32, q.shape, 1)
    zero = jnp.zeros_like(q)
    q2 = jnp.concatenate([jnp.where(lane < 64, q, zero), jnp.where(lane >= 64, q, zero)], axis=0)
    row = lax.broadcasted_iota(jnp.int32, (2 * tq, 1), 0)
    qpos = i * tq + jnp.where(row >= tq, row - tq, row)
    tiles_per_chunk = ck // tq

    def attend(nchunk):
        side_work()
        mx = jnp.full((2 * tq, 1), NEG, F32)
        for j in range(nchunk):
            cols = slice(j * ck, (j + 1) * ck)
            s = _dot(q2, kt_ref[:, cols].astype(BF16))
            if j == nchunk - 1:
                kpos = j * ck + lax.broadcasted_iota(jnp.int32, (1, ck), 1)
                s = jnp.where(kpos <= qpos, s, NEG)
            s_ref[:, cols] = s
            mx = jnp.maximum(mx, jnp.max(s, -1, keepdims=True))
        ones = jnp.ones((ck, LANES), BF16)
        l = jnp.zeros((2 * tq, LANES), F32)
        acc = jnp.zeros((2 * tq, LANES), F32)
        for j in range(nchunk):
            cols = slice(j * ck, (j + 1) * ck)
            p = jnp.exp(s_ref[:, cols] - mx).astype(BF16)
            l = l + _dot(p, ones)
            acc = acc + _dot(p, v_ref[cols, :].astype(BF16))
        o = acc / l
        att = o[:tq] - lam * o[tq:]
        o_ref[...] = (_sub_norm(att, g_ref, lam_init) * gate_ref[...]).astype(o_ref.dtype)

    for c in range(t // ck):
        pl.when(i // tiles_per_chunk == c)(functools.partial(attend, c + 1))


def _diff_attention_body(pt_ref, q_ref, *refs, npg, n_new, nmap, tq, ck, lam_init):
    k_refs = refs[:npg]
    v_refs = refs[npg:2 * npg]
    (knew_ref, vnew_ref, gate_ref, qp_ref, kt_ref, vp_ref, gatep_ref, lq1, lk1, lq2, lk2, g_ref,
     o_ref, op_ref, qbd_ref, m_ref, l_ref, acc_ref, kpad_ref, vpad_ref, s_ref) = refs[2 * npg:]
    c = pl.program_id(1)
    nh = nmap // 2
    dk = SEG // nmap
    rows_h = 2 * n_new
    lam = _diff_lambda(lq1, lk1, lq2, lk2, lam_init)

    @pl.when(c == 0)
    def _():
        qt = jnp.tile(q_ref[...].astype(F32), (nmap, 1))
        rmap = lax.broadcasted_iota(jnp.int32, qt.shape, 0) // n_new
        cmap = lax.broadcasted_iota(jnp.int32, qt.shape, 1) // dk
        qbd_ref[...] = jnp.where(rmap == cmap, qt, 0.0).astype(BF16)
        m_ref[...] = jnp.full(m_ref.shape, NEG, F32)
        l_ref[...] = jnp.zeros(l_ref.shape, F32)
        acc_ref[...] = jnp.zeros(acc_ref.shape, F32)

    def update(s, values_of_head):
        m_prev = m_ref[...]
        m_new = jnp.maximum(m_prev, jnp.max(s, -1, keepdims=True))
        alpha = jnp.exp(m_prev - m_new)
        p = jnp.exp(s - m_new)
        l_ref[...] = alpha * l_ref[...] + jnp.sum(p, -1, keepdims=True)
        m_ref[...] = m_new
        pb = p.astype(BF16)
        for h in range(nh):
            rows = slice(h * rows_h, (h + 1) * rows_h)
            acc_ref[rows, :] = alpha[rows] * acc_ref[rows, :] + _dot(pb[rows, :], values_of_head(h))

    def cache_chunk():
        qbd = qbd_ref[...]
        s = jnp.concatenate([_dot(qbd, k_refs[r][...].astype(BF16)) for r in range(npg)], axis=1)
        update(s, lambda h: jnp.concatenate(
            [v_refs[r][pl.ds(h, PAGE, stride=nh), :].astype(BF16) for r in range(npg)], axis=0))

    _diff_prompt_tile(c, qp_ref, kt_ref, vp_ref, gatep_ref, lam, g_ref, op_ref, s_ref, cache_chunk,
                      tq=tq, ck=ck, lam_init=lam_init)

    @pl.when(c == pl.num_programs(1) - 1)
    def _():
        qbd = qbd_ref[...]
        kpad_ref[...] = jnp.zeros(kpad_ref.shape, F32)
        vpad_ref[...] = jnp.zeros(vpad_ref.shape, F32)
        kpad_ref[0:n_new, :] = knew_ref[...]
        vpad_ref[0:n_new, :] = vnew_ref[...]
        s_new = _dot_nt(qbd, kpad_ref[...].astype(BF16))
        tnew = lax.broadcasted_iota(jnp.int32, s_new.shape, 1)
        qidx = lax.broadcasted_iota(jnp.int32, s_new.shape, 0) & (n_new - 1)
        s_new = jnp.where(tnew <= qidx, s_new, NEG)
        update(s_new, lambda h: vpad_ref[:, h * LANES:(h + 1) * LANES].astype(BF16))
        o = acc_ref[...] / l_ref[...]
        for h in range(nh):
            o1 = o[(2 * h) * n_new:(2 * h + 1) * n_new]
            o2 = o[(2 * h + 1) * n_new:(2 * h + 2) * n_new]
            att = _sub_norm(o1 - lam * o2, g_ref, lam_init)
            cols = slice(h * LANES, (h + 1) * LANES)
            o_ref[:, cols] = (att * gate_ref[:, cols]).astype(o_ref.dtype)


def _diff_attention(q_s, k_new, v_new, gate_s, cache_kt, cache_v, page_table,
                    q_p, kt_p, v_p, gate_p, lam_refs, subln_g, lam_init, n_new, bp, tp):
    nb, n_pages = page_table.shape
    nmap = 16
    npg = 8
    tq, ck = 256, 512
    nh = SEG // LANES
    nchunk = n_pages // npg
    assert bp * nh == nb and tp // tq == nchunk, "one prompt tile per decode chunk"
    r3 = lambda a: a.reshape(nb, n_new, SEG)
    p3 = lambda a: a.reshape(bp, tp, SEG)
    pt = page_table.reshape(-1)
    small = lambda w: pl.BlockSpec((1, w), lambda b, c, pt: (0, 0))
    tok = lambda: pl.BlockSpec((None, n_new, SEG), lambda b, c, pt: (b, 0, 0))
    ptile = lambda: pl.BlockSpec((None, tq, LANES), lambda b, c, pt: (b // nh, c, b % nh))

    def page_spec(r):
        return pl.BlockSpec((None, SEG, LANES), lambda b, c, pt, r=r: (pt[b * n_pages + c * npg + r], 0, 0))

    in_specs = [tok()]
    in_specs += [page_spec(r) for r in range(npg)]
    in_specs += [page_spec(r) for r in range(npg)]
    in_specs += [tok(), tok(), tok(),
                 ptile(),
                 pl.BlockSpec((None, LANES, tp), lambda b, c, pt: (b // nh, b % nh, 0)),
                 pl.BlockSpec((None, tp, LANES), lambda b, c, pt: (b // nh, 0, b % nh)),
                 ptile(),
                 small(64), small(64), small(64), small(64), small(LANES)]
    o_s, o_p = pl.pallas_call(
        functools.partial(_diff_attention_body, npg=npg, n_new=n_new, nmap=nmap, tq=tq, ck=ck, lam_init=lam_init),
        grid_spec=pltpu.PrefetchScalarGridSpec(
            num_scalar_prefetch=1,
            grid=(nb, nchunk),
            in_specs=in_specs,
            out_specs=[tok(), ptile()],
            scratch_shapes=[pltpu.VMEM((nmap * n_new, SEG), BF16),
                            pltpu.VMEM((nmap * n_new, 1), F32), pltpu.VMEM((nmap * n_new, 1), F32),
                            pltpu.VMEM((nmap * n_new, LANES), F32),
                            pltpu.VMEM((PAGE, SEG), F32), pltpu.VMEM((PAGE, SEG), F32),
                            pltpu.VMEM((2 * tq, tp), F32)],
        ),
        out_shape=[jax.ShapeDtypeStruct((nb, n_new, SEG), BF16), jax.ShapeDtypeStruct((bp, tp, SEG), BF16)],
        compiler_params=_cparams(("parallel", "arbitrary")),
        name="diff_attention",
    )(pt, r3(q_s), *([cache_kt] * npg), *([cache_v] * npg), r3(k_new), r3(v_new), r3(gate_s),
      p3(q_p), kt_p, p3(v_p), p3(gate_p), *lam_refs, subln_g)
    return o_p.reshape(bp * tp, SEG), o_s.reshape(nb * n_new, SEG)


def _out_ln_body(*refs, nmix, alpha):
    mix_refs = refs[:nmix]
    w_ref, x_ref, g_ref, b_ref, o_ref, ob_ref = refs[nmix:]
    f = _dot(mix_refs[0][...], w_ref[0:SEG, :])
    for s in range(1, nmix):
        f = f + _dot(mix_refs[s][...], w_ref[s * SEG:(s + 1) * SEG, :])
    z = alpha * x_ref[...] + f
    mu = jnp.mean(z, -1, keepdims=True)
    cen = z - mu
    var = jnp.mean(cen * cen, -1, keepdims=True)
    y = cen * lax.rsqrt(var + LN_EPS) * g_ref[...] + b_ref[...]
    o_ref[...] = y
    ob_ref[...] = y.astype(BF16)


def _out_ln(mixes, w, x, g, b, alpha, tm, name):
    n, d = x.shape
    nmix = len(mixes)
    in_specs = [pl.BlockSpec((tm, SEG), lambda i: (i, 0)) for _ in mixes]
    in_specs += [
        pl.BlockSpec((nmix * SEG, d), lambda i: (0, 0)),
        pl.BlockSpec((tm, d), lambda i: (i, 0)),
        pl.BlockSpec((1, d), lambda i: (0, 0)),
        pl.BlockSpec((1, d), lambda i: (0, 0)),
    ]
    return pl.pallas_call(
        functools.partial(_out_ln_body, nmix=nmix, alpha=alpha),
        grid=(n // tm,),
        in_specs=in_specs,
        out_specs=[pl.BlockSpec((tm, d), lambda i: (i, 0)), pl.BlockSpec((tm, d), lambda i: (i, 0))],
        out_shape=[jax.ShapeDtypeStruct((n, d), F32), jax.ShapeDtypeStruct((n, d), BF16)],
        compiler_params=_cparams(("parallel",)),
        name=name,
    )(*mixes, w, x, g, b)


def _dilated_prompt_group(q_ref, k_ref, v_ref, og_ref, ls_ref, win, dil, blk):
    t = q_ref.shape[0]
    assert win // dil == blk

    def rows(ref, start, n):
        if dil == 1:
            return ref[start:start + n, :]
        return ref[pl.ds(start, n, stride=dil), :]

    def put(ref, start, n, val):
        if dil == 1:
            ref[start:start + n, :] = val
        else:
            ref[pl.ds(start, n, stride=dil), :] = val

    for r in range(dil):
        for i in range(t // dil // blk):
            base = r + dil * blk * i
            q = rows(q_ref, base, blk).astype(BF16)
            kstart, nk = (r, blk) if i == 0 else (base - dil * blk, 2 * blk)
            kb = rows(k_ref, kstart, nk).astype(BF16)
            vb = rows(v_ref, kstart, nk).astype(BF16)
            s = _dot_nt(q, kb)
            a = lax.broadcasted_iota(jnp.int32, s.shape, 0)
            b = lax.broadcasted_iota(jnp.int32, s.shape, 1)
            ok = (b <= a) if i == 0 else ((b >= a) & (b <= a + blk))
            s = jnp.where(ok, s, NEG)
            m = jnp.max(s, -1, keepdims=True)
            p = jnp.exp(s - m)
            l = jnp.sum(p, -1, keepdims=True)
            put(og_ref, base, blk, _dot(p.astype(BF16), vb) / l)
            put(ls_ref, base, blk, jnp.broadcast_to(m + jnp.log(l), (blk, LANES)))


def _dilated_prompt_merge(ogs, lss, gate_ref, o_ref):
    la, lb, lc = lss[0][...], lss[1][...], lss[2][...]
    top = jnp.maximum(jnp.maximum(la, lb), lc)
    wa, wb, wc = jnp.exp(la - top), jnp.exp(lb - top), jnp.exp(lc - top)
    o = (ogs[0][...] * wa + ogs[1][...] * wb + ogs[2][...] * wc) / (wa + wb + wc)
    o_ref[...] = (o * gate_ref[...]).astype(o_ref.dtype)


def _diag_mask(g, per):
    sub = lax.broadcasted_iota(jnp.int32, (g, LANES), 0)
    lane = lax.broadcasted_iota(jnp.int32, (g, LANES), 1)
    return sub == lane // per


def _diag_row(x, diag):
    return jnp.sum(jnp.where(diag, x, 0.0), axis=0, keepdims=True)


def _dilated_sample_body(qc_ref, buf_ref, nxt_ref, new_ref, *rest, rows, n_new, nh, dil, fused):
    if fused:
        (q0, q1, q2, k0, v0, k1, v1, k2, v2, gatep_ref, o_ref, lse_ref, nbuf_hbm, op_ref,
         m_ref, l_ref, acc_ref, sem, og0, og1, og2, ls0, ls1, ls2) = rest
    else:
        o_ref, lse_ref, nbuf_hbm, m_ref, l_ref, acc_ref, sem = rest
    b = pl.program_id(0)
    c = pl.program_id(1)
    last = c == pl.num_programs(1) - 1
    diag = _diag_mask(nh, n_new)
    qc = qc_ref[...]

    body_copy = pltpu.make_async_copy(buf_ref.at[0, pl.ds(n_new, rows - n_new)],
                                      nbuf_hbm.at[b, pl.ds(c * rows, rows - n_new)], sem.at[0])
    tail_dst = nbuf_hbm.at[b, pl.ds(c * rows + rows - n_new, n_new)]
    tail_new = pltpu.make_async_copy(new_ref.at[0], tail_dst, sem.at[1])
    tail_nxt = pltpu.make_async_copy(nxt_ref.at[0], tail_dst, sem.at[1])
    body_copy.start()
    pl.when(last)(tail_new.start)
    pl.when(jnp.logical_not(last))(tail_nxt.start)

    @pl.when(c == 0)
    def _():
        m_ref[...] = jnp.where(diag, -jnp.inf, 0.0)
        l_ref[...] = jnp.zeros(l_ref.shape, F32)
        acc_ref[...] = jnp.zeros(acc_ref.shape, F32)

    def update(k3, v3, mask3):
        n_t = k3.shape[0]
        n = n_t * nh
        s3 = _dot(k3.reshape(n, LANES).astype(BF16), qc).reshape(n_t, nh, LANES)
        s3 = jnp.where(mask3, s3, -jnp.inf)
        m_old = m_ref[...]
        m_new = jnp.maximum(m_old, jnp.max(s3, axis=0))
        m_safe = jnp.where(m_new == -jnp.inf, 0.0, m_new)
        alpha = jnp.exp(m_old - m_safe)
        p3 = jnp.exp(s3 - m_safe[None])
        l_ref[...] = alpha * l_ref[...] + jnp.sum(p3, axis=0)
        m_ref[...] = m_new
        v2 = v3.reshape(n, LANES).astype(BF16)
        acc_ref[...] = acc_ref[...] * _diag_row(alpha, diag) + _dot_tn(v2, p3.reshape(n, LANES).astype(BF16))

    def buffer_chunk():
        k3, v3 = buf_ref[0, :, 0:nh, :], buf_ref[0, :, nh:2 * nh, :]
        idx = lax.broadcasted_iota(jnp.int32, (rows, nh, LANES), 0)
        if dil > n_new:
            keep = lambda a: a.reshape(rows // dil, dil, nh, LANES)[:, 0:n_new].reshape(rows // dil * n_new, nh, LANES)
            k3, v3, idx = keep(k3), keep(v3), idx[0:rows // dil * n_new]
            idx = (idx // n_new) * dil + (idx & (n_new - 1))
        rho = c * rows + idx
        tq = lax.broadcasted_iota(jnp.int32, rho.shape, 2) & (n_new - 1)
        seen = (rho >= tq) & (((rho - tq) & (dil - 1)) == 0)
        update(k3, v3, diag[None] & seen)

    if not fused:
        buffer_chunk()
    else:
        blk = C_WINDOWS[0] // C_DILATIONS[0]
        groups = ((q0, k0, v0, og0, ls0), (q1, k1, v1, og1, ls1), (q2, k2, v2, og2, ls2))

        def with_group(g):
            buffer_chunk()
            _dilated_prompt_group(*groups[g], C_WINDOWS[g], C_DILATIONS[g], blk)

        def with_merge():
            buffer_chunk()
            _dilated_prompt_merge((og0, og1, og2), (ls0, ls1, ls2), gatep_ref, op_ref)

        for g in range(len(groups)):
            pl.when(c == g)(functools.partial(with_group, g))
        pl.when(c == len(groups))(with_merge)

    @pl.when(last)
    def _():
        tn = lax.broadcasted_iota(jnp.int32, (n_new, nh, LANES), 0)
        tl = lax.broadcasted_iota(jnp.int32, (n_new, nh, LANES), 2) & (n_new - 1)
        dist = tl - tn
        ok = (dist >= 0) & ((dist & (dil - 1)) == 0)
        update(new_ref[0, :, 0:nh, :], new_ref[0, :, nh:2 * nh, :], diag[None] & ok)
        lane = lax.broadcasted_iota(jnp.int32, (1, LANES), 1)
        used = lane < nh * n_new
        l_row = jnp.where(used, _diag_row(l_ref[...], diag), 1.0)
        m_row = jnp.where(used, _diag_row(m_ref[...], diag), 0.0)
        o = (acc_ref[...] / l_row).T
        lse = jnp.broadcast_to(m_row + jnp.log(l_row), (LANES, LANES)).T
        for h in range(nh):
            cols = slice(h * LANES, (h + 1) * LANES)
            o_ref[:, cols] = o[h * n_new:(h + 1) * n_new]
            lse_ref[:, cols] = lse[h * n_new:(h + 1) * n_new]

    body_copy.wait()
    pl.when(last)(tail_new.wait)
    pl.when(jnp.logical_not(last))(tail_nxt.wait)


def _dilated_sample(q, buf4, new4, dil, prompt=None):
    nb, lb = buf4.shape[0], buf4.shape[1]
    n_new = new4.shape[1]
    nh = buf4.shape[2] // 2
    rows = min(lb, 512)
    nch = lb // rows
    qc = q.reshape(nb, n_new, nh, LANES).transpose(0, 3, 2, 1).reshape(nb, LANES, nh * n_new)
    qc = jnp.pad(qc, ((0, 0), (0, 0), (0, LANES - nh * n_new)))
    ospec = lambda: pl.BlockSpec((None, n_new, SEG), lambda b, c: (b, 0, 0))
    per = rows // n_new
    in_specs = [
        pl.BlockSpec((None, LANES, LANES), lambda b, c: (b, 0, 0)),
        pl.BlockSpec((1, rows, 2 * nh, LANES), lambda b, c: (b, c, 0, 0)),
        pl.BlockSpec((1, n_new, 2 * nh, LANES),
                     lambda b, c: (b, jnp.minimum((c + 1) * per, nch * per - 1), 0, 0)),
        pl.BlockSpec((1, n_new, 2 * nh, LANES), lambda b, c: (b, 0, 0, 0)),
    ]
    args = [qc, buf4, buf4, new4]
    out_specs = [ospec(), ospec(), pl.BlockSpec(memory_space=pl.ANY)]
    out_shape = [jax.ShapeDtypeStruct((nb, n_new, SEG), F32), jax.ShapeDtypeStruct((nb, n_new, SEG), F32),
                 jax.ShapeDtypeStruct(buf4.shape, F32)]
    scratch = [pltpu.VMEM((nh, LANES), F32), pltpu.VMEM((nh, LANES), F32),
               pltpu.VMEM((LANES, LANES), F32), pltpu.SemaphoreType.DMA((2,))]
    if prompt is not None:
        qs, kvs, gate, bp, tp = prompt
        assert bp * nh == nb and nch == len(qs) + 1, "one prompt (batch, head) per sequence, one group per chunk"
        hspec = lambda off=0: pl.BlockSpec((None, tp, LANES), lambda b, c: (b // nh, 0, off + b % nh))
        kv3 = [kv.reshape(bp, tp, 2 * SEG) for kv in kvs]
        in_specs += [hspec(), hspec(), hspec(), hspec(), hspec(nh), hspec(), hspec(nh), hspec(), hspec(nh), hspec()]
        args += [a.reshape(bp, tp, SEG) for a in qs]
        args += [kv3[0], kv3[0], kv3[1], kv3[1], kv3[2], kv3[2], gate.reshape(bp, tp, SEG)]
        out_specs.append(hspec())
        out_shape.append(jax.ShapeDtypeStruct((bp, tp, SEG), BF16))
        scratch += [pltpu.VMEM((tp, LANES), F32) for _ in range(6)]
    return pl.pallas_call(
        functools.partial(_dilated_sample_body, rows=rows, n_new=n_new, nh=nh, dil=dil, fused=prompt is not None),
        grid=(nb, nch),
        in_specs=in_specs,
        out_specs=out_specs,
        out_shape=out_shape,
        scratch_shapes=scratch,
        compiler_params=_cparams(("parallel", "arbitrary")),
        name=f"dilated_sample_d{dil}",
    )(*args)


def _merge_sample_body(o0, o1, o2, l0, l1, l2, gate_ref, o_ref):
    a, b, c = l0[...], l1[...], l2[...]
    top = jnp.maximum(jnp.maximum(a, b), c)
    wa, wb, wc = jnp.exp(a - top), jnp.exp(b - top), jnp.exp(c - top)
    o = (o0[...] * wa + o1[...] * wb + o2[...] * wc) / (wa + wb + wc)
    o_ref[...] = (o * gate_ref[...]).astype(o_ref.dtype)


def _merge_sample(outs, lses, gate):
    n = gate.shape[0]
    spec = lambda: pl.BlockSpec((n, SEG), lambda i: (0, 0))
    flat = lambda a: a.reshape(n, SEG)
    return pl.pallas_call(
        _merge_sample_body,
        grid=(1,),
        in_specs=[spec() for _ in range(7)],
        out_specs=spec(),
        out_shape=jax.ShapeDtypeStruct((n, SEG), BF16),
        compiler_params=_cparams(("arbitrary",)),
        name="merge_sample",
    )(*[flat(o) for o in outs], *[flat(l) for l in lses], gate)


def kernel(x_prompt, x_sample, cache_kb, cache_vb, state_conv, state_kv_c0, state_kv_c1, state_kv_c2,
           page_table, w_in_ab, b_in_ab, w_dw, b_dw, ln_a_g, ln_a_b, lam_q1, lam_k1, lam_q2, lam_k2,
           subln_g, w_out_ab, w_in_c, b_in_c, w_out_c, post_ln_g, post_ln_b):
    bp, tp, d = x_prompt.shape
    bs, ts, _ = x_sample.shape
    n_p, n_s = bp * tp, bs * ts
    n_phys = cache_kb.shape[1]
    past_len = page_table.shape[1] * PAGE
    depth = post_ln_g.shape[0]
    alpha = (2 * depth) ** 0.25
    states_c = (state_kv_c0, state_kv_c1, state_kv_c2)
    tm_p, tm_s = 512, n_s
    row = lambda a: a.reshape(1, -1)

    pos_p = jnp.arange(tp)
    pos_s = jnp.tile(past_len + jnp.arange(ts), bs)
    tabs = {(64, "p"): _rope_tables(pos_p, 64), (64, "s"): _rope_tables(pos_s, 64),
            (128, "p"): _rope_tables(pos_p, 128), (128, "s"): _rope_tables(pos_s, 128)}

    xp32 = x_prompt.reshape(n_p, d)
    xs32 = x_sample.reshape(n_s, d)

    lam_init = 0.8 - 0.6 * math.exp(-0.3 * 0)
    w_in = w_in_ab[0].astype(BF16)
    b_in = row(b_in_ab[0])
    w_out = w_out_ab[0].astype(BF16)
    lam_refs = [row(lam_q1[0]), row(lam_k1[0]), row(lam_q2[0]), row(lam_k2[0])]
    sub_g = row(subln_g[0])
    conv_args = (w_dw[0], row(b_dw[0]), row(ln_a_g[0]), row(ln_a_b[0]))
    kseg = 4

    pr = functools.partial(_proj, xp32, w_in, b_in, tm=tm_p, dh=64)
    glu = pr(seg0=0, kinds=("glu_val", "glu_gate"), out_dtype=F32, name="proj_glu_p")
    g_a = pr(seg0=2, kinds=("silu",), out_dtype=F32, name="proj_ga_p")
    q = pr(seg0=3, kinds=("rope_q",), out_dtype=BF16, tables=tabs[(64, "p")], name="proj_qb_p")
    v = pr(seg0=5, kinds=("none",), out_dtype=F32, name="proj_vb_p")
    g_b = pr(seg0=6, kinds=("silu",), out_dtype=F32, name="proj_gb_p")
    kt = _proj_kt(xp32, w_in[:, kseg * SEG:(kseg + 1) * SEG].T, b_in_ab[0, kseg * SEG:(kseg + 1) * SEG].reshape(SEG, 1),
                  pos_p, 64, bp, tp, tm_p, "proj_kbt_p")
    a_mix = _conv_prompt(glu, g_a, *conv_args, bp, tp)
    kb_p = kt.reshape(1, bp, 16, 64, tp).transpose(0, 1, 4, 2, 3)
    vb_p = v.reshape(1, bp, tp, 8, 128)
    conv_p = glu.reshape(bp, tp, SEG)[None, :, tp - (CONV_W - 1):]

    slab = _proj_multi(xs32, w_in, b_in, ("glu_val", "glu_gate", "silu", "rope_q", "rope", "none", "silu"),
                       tabs[(64, "s")], 64, "proj_ab_s")
    seg = lambda k: slab[:, k * SEG:(k + 1) * SEG]
    glu, g_a, q_s, k_s, v_s, gb_s = seg(1), seg(2), seg(3), seg(kseg), seg(5), seg(6)
    xpad = jnp.concatenate([state_conv[0], glu.reshape(bs, ts, SEG)], axis=1)
    a_mix_s = _conv_sample(xpad, g_a, *conv_args)
    cache_kt = cache_kb[0].transpose(0, 2, 3, 1).reshape(n_phys, SEG, PAGE)
    cache_v = cache_vb[0].reshape(n_phys, PAGE * 8, LANES)
    b_mix, b_mix_s = _diff_attention(q_s, k_s, v_s, gb_s, cache_kt, cache_v, page_table,
                                     q, kt, v, g_b, lam_refs, sub_g, lam_init, ts, bp, tp)
    xp32, xp16 = _out_ln([a_mix, b_mix], w_out, xp32, row(post_ln_g[0]), row(post_ln_b[0]), alpha, tm_p, "out_ab_p")
    xs32, xs16 = _out_ln([a_mix_s, b_mix_s], w_out, xs32, row(post_ln_g[0]), row(post_ln_b[0]), alpha, tm_s,
                         "out_ab_s")
    kb_s = k_s.reshape(1, bs, ts, 16, 64)
    vb_s = v_s.reshape(1, bs, ts, 8, 128)
    conv_s = xpad[None, :, ts:]

    w_in = w_in_c[0].astype(BF16)
    b_in = row(b_in_c[0])
    w_out = w_out_c[0].astype(BF16)
    ngrp = len(C_WINDOWS)

    pr = functools.partial(_proj, xp16, w_in, b_in, tm=tm_p, dh=128, tables=tabs[(128, "p")])
    qs_p = [pr(seg0=3 * g, kinds=("rope_q",), out_dtype=F32, name=f"proj_qc{g}_p") for g in range(ngrp)]
    kvs_p = [pr(seg0=3 * g + 1, kinds=("rope", "none"), out_dtype=F32, name=f"proj_kvc{g}_p") for g in range(ngrp)]
    gate_p = _proj(xp16, w_in, b_in, seg0=3 * ngrp, kinds=("silu",), out_dtype=F32, tm=tm_p, name="proj_gc_p")
    kvc_p = [kv.reshape(bp, tp, 2, 8, 128)[None, :, tp - min(w, tp):] for kv, w in zip(kvs_p, C_WINDOWS)]

    slab = _proj_multi(xs16, w_in, b_in, ("rope_q", "rope", "none") * ngrp + ("silu",), tabs[(128, "s")], 128,
                       "proj_c_s")
    qs = [slab[:, 3 * g * SEG:(3 * g + 1) * SEG].astype(BF16) for g in range(ngrp)]
    kvs = [slab[:, (3 * g + 1) * SEG:(3 * g + 3) * SEG] for g in range(ngrp)]
    gate = slab[:, 3 * ngrp * SEG:]
    outs, lses, kvc_s = [], [], []
    for g in range(ngrp):
        buf = states_c[g][0]
        lb = buf.shape[1]
        prompt = (qs_p, kvs_p, gate_p, bp, tp) if g == ngrp - 1 else None
        res = _dilated_sample(qs[g], buf.reshape(bs, lb, 16, LANES), kvs[g].reshape(bs, ts, 16, LANES),
                              C_DILATIONS[g], prompt)
        outs.append(res[0])
        lses.append(res[1])
        kvc_s.append(res[2].reshape(1, bs, lb, 2, 8, 128))
    o_mix_p = res[3].reshape(n_p, SEG)
    y_p, _ = _out_ln([o_mix_p], w_out, xp32, row(post_ln_g[1]), row(post_ln_b[1]), alpha, tm_p, "out_c_p")
    o_mix = _merge_sample(outs, lses, gate)
    y_s, _ = _out_ln([o_mix], w_out, xs32, row(post_ln_g[1]), row(post_ln_b[1]), alpha, tm_s, "out_c_s")

    return (y_p.reshape(bp, tp, d), y_s.reshape(bs, ts, d),
            kb_p, vb_p, conv_p, kb_s, vb_s, conv_s,
            kvc_p[0], kvc_p[1], kvc_p[2], kvc_s[0], kvc_s[1], kvc_s[2])
```
